```python
import jax, jax.numpy as jnp
from jax import lax
import numpy as np

D_MODEL = 1024
BATCH = 8
SEQ = 4096
DEPTH = 2
DEC_BATCH = 32
DEC_SEQ = 8
PAST_LEN = 16384
PAGE_SIZE = 128

HEAD_DIM = 64
H_SB = 4
H_FOX = 4
H_NSA = 8
HKV_NSA = 2
GQA_R = H_NSA // HKV_NSA
ROPE_DIM = HEAD_DIM // 4
ROPE_THETA = 500000.0
CMP_BLOCK = 32
SEL_BLOCK = 64
SEL_TOPK = 16
WINDOW = 512
D_FF = 3 * D_MODEL
CONV_W = 3
QBLOCK = 128
SEL_QBLOCK = 32
EPS = 1e-6
SCALE = HEAD_DIM ** -0.5
NEG_INF = -1e30
TINY = 1e-30
FORCE_SCORE = 1e6

IN_SPLITS = (
    ("sb_q", H_SB * HEAD_DIM), ("sb_k", H_SB * HEAD_DIM), ("sb_v", H_SB * HEAD_DIM),
    ("fox_q", H_FOX * HEAD_DIM), ("fox_k", H_FOX * HEAD_DIM), ("fox_v", H_FOX * HEAD_DIM), ("fox_f", H_FOX),
    ("nsa_q", H_NSA * HEAD_DIM),
    ("cmp_k", HKV_NSA * HEAD_DIM), ("cmp_v", HKV_NSA * HEAD_DIM),
    ("sel_k", HKV_NSA * HEAD_DIM), ("sel_v", HKV_NSA * HEAD_DIM),
    ("win_k", HKV_NSA * HEAD_DIM), ("win_v", HKV_NSA * HEAD_DIM),
    ("nsa_gate", 3 * H_NSA),
    ("merge_gate", 3 * D_MODEL),
)
N_IN = sum(w for _, w in IN_SPLITS)

kernel_name = "hybrid_sb_fox_nsa_decoder_step"

F32 = jnp.float32


def rms_norm(x, g):
    xf = x.astype(F32)
    y = xf * lax.rsqrt(jnp.mean(xf * xf, axis=-1, keepdims=True) + EPS) * g.astype(F32)
    return y.astype(x.dtype)


def partial_rope(x, pos):
    half = ROPE_DIM // 2
    inv = ROPE_THETA ** (-jnp.arange(half, dtype=F32) * (2.0 / ROPE_DIM))
    ang = pos.astype(F32)[:, None] * inv[None, :]
    cos = jnp.cos(ang)[:, None, :]
    sin = jnp.sin(ang)[:, None, :]
    xf = x.astype(F32)
    x1, x2 = xf[..., :half], xf[..., half:ROPE_DIM]
    out = jnp.concatenate([x1 * cos - x2 * sin, x2 * cos + x1 * sin, xf[..., ROPE_DIM:]], axis=-1)
    return out.astype(x.dtype)


def query_blocks(fn, q_pos, *q_args, block=QBLOCK):
    n_q = q_pos.shape[0]
    qb = block if n_q % block == 0 else n_q
    nb = n_q // qb
    split = lambda a: jnp.moveaxis(a.reshape(a.shape[0], nb, qb, *a.shape[2:]), 1, 0)
    out = lax.map(lambda xs: fn(*xs), (q_pos.reshape(nb, qb),) + tuple(split(a) for a in q_args))
    out = jnp.moveaxis(out, 0, 1)
    return out.reshape(out.shape[0], n_q, *out.shape[3:])


def stick_breaking_block(q, k, v, q_pos, k_pos):
    z = jnp.einsum('bqhd,bkhd->bhqk', q, k, preferred_element_type=F32) * SCALE
    mask = k_pos[None, :] < q_pos[:, None]
    log_keep = jnp.where(mask, jax.nn.log_sigmoid(-z), 0.0)
    after = lax.cumsum(log_keep, axis=3, reverse=True) - log_keep
    a = jnp.where(mask, jnp.exp(jax.nn.log_sigmoid(z) + after), 0.0)
    return jnp.einsum('bhqk,bkhd->bqhd', a, v.astype(F32))


def forgetting_block(q, k, v, fq, fk, q_pos, k_pos):
    s = jnp.einsum('bqhd,bkhd->bhqk', q, k, preferred_element_type=F32) * SCALE
    s = s + jnp.swapaxes(fq, 1, 2)[..., None] - jnp.swapaxes(fk, 1, 2)[:, :, None, :]
    mask = k_pos[None, :] <= q_pos[:, None]
    p = jax.nn.softmax(jnp.where(mask, s, NEG_INF), axis=-1)
    return jnp.einsum('bhqk,bkhd->bqhd', p, v.astype(F32))


def window_attn(q, q_pos, k, v, k_pos):
    s = jnp.einsum('bqgrd,bkgd->bqgrk', q, k, preferred_element_type=F32) * SCALE
    diff = q_pos[:, None] - k_pos[None, :]
    mask = ((diff >= 0) & (diff < WINDOW) & (k_pos[None, :] >= 0))[None, :, None, None, :]
    p = jax.nn.softmax(jnp.where(mask, s, NEG_INF), axis=-1)
    return jnp.einsum('bqgrk,bkgd->bqgrd', p, v.astype(F32))


def window_banded(q, q_pos, k, v):
    padw = ((0, 0), (WINDOW, 0), (0, 0), (0, 0))
    kp, vp = jnp.pad(k, padw), jnp.pad(v, padw)

    def blk(qp, qb):
        span = qb.shape[1] + WINDOW
        start = qp[0]
        ks = lax.dynamic_slice_in_dim(kp, start, span, axis=1)
        vs = lax.dynamic_slice_in_dim(vp, start, span, axis=1)
        kpos = start - WINDOW + jnp.arange(span, dtype=jnp.int32)
        return window_attn(qb, qp, ks, vs, kpos)

    return query_blocks(blk, q_pos, q, block=QBLOCK)


def nsa_selected_block(q, idx, q_pos, ks, vs):
    B, G = ks.shape[0], ks.shape[1]
    bi = jnp.arange(B)[:, None, None, None]
    gi = jnp.arange(G)[None, None, :, None]
    kg = ks[bi, gi, idx]
    vg = vs[bi, gi, idx]
    kpos = idx[..., None] * SEL_BLOCK + jnp.arange(SEL_BLOCK)
    mask = (kpos <= q_pos[None, :, None, None, None])[:, :, :, None]
    s = jnp.einsum('bqgrd,bqgksd->bqgrks', q, kg, preferred_element_type=F32) * SCALE
    s = jnp.where(mask, s, NEG_INF)
    shp = s.shape
    p = jax.nn.softmax(s.reshape(*shp[:4], -1), axis=-1).reshape(shp)
    return jnp.einsum('bqgrks,bqgksd->bqgrd', p, vg.astype(F32))


def nsa_sparse(q, q_pos, cmp_k, cmp_v, sel_k, sel_v, pool):
    B, L = cmp_k.shape[0], cmp_k.shape[1]
    l_pad = -(-L // SEL_BLOCK) * SEL_BLOCK
    padw = ((0, 0), (0, l_pad - L), (0, 0), (0, 0))
    ck, cv, sk, sv = (jnp.pad(a, padw) for a in (cmp_k, cmp_v, sel_k, sel_v))
    n_c, n_s = l_pad // CMP_BLOCK, l_pad // SEL_BLOCK
    kc = jnp.einsum('bnchd,ch->bnhd', ck.reshape(B, n_c, CMP_BLOCK, HKV_NSA, HEAD_DIM), pool[0])
    vc = jnp.einsum('bnchd,ch->bnhd', cv.reshape(B, n_c, CMP_BLOCK, HKV_NSA, HEAD_DIM), pool[1])
    s = jnp.einsum('bqgrd,bngd->bqgrn', q, kc, preferred_element_type=F32) * SCALE
    c_end = (jnp.arange(n_c) + 1) * CMP_BLOCK - 1
    cmask = (c_end[None, :] <= q_pos[:, None])[None, :, None, None, :]
    s = jnp.where(cmask, s, NEG_INF)
    e = jnp.where(cmask, jnp.exp(s - jnp.max(s, axis=-1, keepdims=True)), 0.0)
    p = e / jnp.maximum(jnp.sum(e, axis=-1, keepdims=True), TINY)
    o_cmp = jnp.einsum('bqgrn,bngd->bqgrd', p, vc.astype(F32))
    imp = p.sum(axis=3).reshape(B, q.shape[1], HKV_NSA, n_s, SEL_BLOCK // CMP_BLOCK).sum(-1)
    blk = jnp.arange(n_s)[None, :]
    cur = (q_pos // SEL_BLOCK)[:, None]
    forced = ((blk == 0) | (blk == cur) | (blk == cur - 1))[None, :, None, :]
    valid = (blk * SEL_BLOCK <= q_pos[:, None])[None, :, None, :]
    score = jnp.where(forced, FORCE_SCORE, jnp.where(valid, imp, -1.0))
    idx = lax.top_k(score, min(SEL_TOPK, n_s))[1]
    ks = sk.reshape(B, n_s, SEL_BLOCK, HKV_NSA, HEAD_DIM).transpose(0, 3, 1, 2, 4)
    vs = sv.reshape(B, n_s, SEL_BLOCK, HKV_NSA, HEAD_DIM).transpose(0, 3, 1, 2, 4)
    o_sel = query_blocks(lambda qp, qb, ib: nsa_selected_block(qb, ib, qp, ks, vs),
                         q_pos, q, idx, block=SEL_QBLOCK)
    return o_cmp, o_sel


def head_rms(x, g):
    xf = x.astype(F32)
    y = xf * lax.rsqrt(jnp.mean(xf * xf, axis=-1, keepdims=True) + EPS) * g.astype(F32)
    return y.astype(x.dtype)


def project(h, pos, lp):
    B, T, _ = h.shape
    z = h @ lp['w_in']
    c, off = {}, 0
    for name, w in IN_SPLITS:
        c[name] = z[..., off:off + w]
        off += w
    heads = lambda a, n: a.reshape(B, T, n, HEAD_DIM)
    r = {}
    r['sb_q'], r['sb_k'], r['sb_v'] = (heads(c[n], H_SB) for n in ('sb_q', 'sb_k', 'sb_v'))
    r['fox_q'] = head_rms(heads(c['fox_q'], H_FOX), lp['fox_qk_gain'][0])
    r['fox_k'] = head_rms(heads(c['fox_k'], H_FOX), lp['fox_qk_gain'][1])
    r['fox_v'] = heads(c['fox_v'], H_FOX)
    r['fox_logf'] = jax.nn.log_sigmoid(c['fox_f'].astype(F32) + lp['fox_forget_bias'].astype(F32))
    q = partial_rope(head_rms(heads(c['nsa_q'], H_NSA), lp['nsa_qk_gain'][0]), pos)
    r['nsa_q'] = q.reshape(B, T, HKV_NSA, GQA_R, HEAD_DIM)
    for i, br in enumerate(('cmp', 'sel', 'win')):
        r[br + '_k'] = partial_rope(head_rms(heads(c[br + '_k'], HKV_NSA), lp['nsa_qk_gain'][1 + i]), pos)
        r[br + '_v'] = heads(c[br + '_v'], HKV_NSA)
    r['nsa_gate'] = jax.nn.sigmoid(c['nsa_gate'].reshape(B, T, 3, HKV_NSA, GQA_R))[..., None]
    r['merge_gate'] = jax.nn.sigmoid(c['merge_gate'].reshape(B, T, 3, D_MODEL))
    return r


def trunk_layer(x, pos, lp, past):
    B, T, _ = x.shape
    r = project(rms_norm(x, lp['attn_norm']), pos, lp)
    if past is None:
        cat = lambda name, new: new
    else:
        cat = lambda name, new: jnp.concatenate([past[name], new], axis=1)
    sb_new = jnp.stack([r['sb_k'], r['sb_v']], axis=2)
    fox_new = jnp.stack([r['fox_k'], r['fox_v']], axis=2)
    cmp_new = jnp.stack([r['cmp_k'], r['cmp_v']], axis=2)
    sel_new = jnp.stack([r['sel_k'], r['sel_v']], axis=2)
    win_new = jnp.stack([r['win_k'], r['win_v']], axis=2)
    sb_all, fox_all = cat('sb_kv', sb_new), cat('fox_kv', fox_new)
    logf_all = cat('fox_logf', r['fox_logf'])
    cmp_all, sel_all = cat('cmp_kv', cmp_new), cat('sel_kv', sel_new)
    L = sb_all.shape[1]
    k_pos = jnp.arange(L, dtype=jnp.int32)
    sb_k, sb_v = sb_all[:, :, 0], sb_all[:, :, 1]
    o_sb = query_blocks(lambda qp, q: stick_breaking_block(q, sb_k, sb_v, qp, k_pos), pos, r['sb_q'])
    fox_k, fox_v = fox_all[:, :, 0], fox_all[:, :, 1]
    f_cum = lax.cumsum(logf_all.astype(F32), axis=1)
    o_fox = query_blocks(lambda qp, q, fq: forgetting_block(q, fox_k, fox_v, fq, f_cum, qp, k_pos),
                         pos, r['fox_q'], f_cum[:, L - T:])
    o_cmp, o_sel = nsa_sparse(r['nsa_q'], pos, cmp_all[:, :, 0], cmp_all[:, :, 1],
                              sel_all[:, :, 0], sel_all[:, :, 1], lp['nsa_cmp_pool'])
    if past is None:
        o_win = window_banded(r['nsa_q'], pos, r['win_k'], r['win_v'])
        win_state = win_new[:, -min(WINDOW, T):]
    else:
        win_all = jnp.concatenate([past['win_kv'], win_new], axis=1)
        n_w = win_all.shape[1]
        w_pos = L - n_w + jnp.arange(n_w, dtype=jnp.int32)
        o_win = window_attn(r['nsa_q'], pos, win_all[:, :, 0], win_all[:, :, 1], w_pos)
        win_state = win_all[:, -past['win_kv'].shape[1]:]
    g = r['nsa_gate']
    o_nsa = g[:, :, 0] * o_cmp + g[:, :, 1] * o_sel + g[:, :, 2] * o_win
    mg = r['merge_gate']
    branch = lambda o, w: o.reshape(B, T, -1).astype(x.dtype) @ w
    mixed = (mg[:, :, 0] * branch(o_sb, lp['w_br_sb']) + mg[:, :, 1] * branch(o_fox, lp['w_br_fox'])
             + mg[:, :, 2] * branch(o_nsa, lp['w_br_nsa']))
    x = x + mixed @ lp['w_o']
    gate, up = jnp.split(rms_norm(x, lp['ffn_norm']) @ lp['w_up'], 2, axis=-1)
    hist = jnp.zeros((B, CONV_W - 1, D_FF), gate.dtype) if past is None else past['conv']
    gp = jnp.concatenate([hist, gate], axis=1)
    conv = lp['conv_b']
    for i in range(CONV_W):
        conv = conv + gp[:, i:i + T] * lp['conv_w'][i]
    y = x + (jax.nn.silu(conv) * up) @ lp['w_down']
    conv_state = gp[:, -(CONV_W - 1):]
    return y, (sb_new, fox_new, r['fox_logf'], cmp_new, sel_new, win_state, conv_state)


def setup_inputs(seed: int = 0) -> dict:
    key = jax.random.key(seed)
    ks = jax.random.split(key, 32)
    n_pages = PAST_LEN // PAGE_SIZE
    n_used = DEC_BATCH * n_pages
    n_pool = n_used + max(1, n_used // 4)
    win_buf = min(WINDOW, PAST_LEN)
    nrm = lambda k, shape, scale=1.0: scale * jax.random.normal(k, shape, F32)
    page_table = jax.random.permutation(ks[0], n_pool)[:n_used].reshape(DEC_BATCH, n_pages).astype(jnp.int32)
    return {
        "x_prompt": nrm(ks[1], (BATCH, SEQ, D_MODEL)),
        "x_sample": nrm(ks[2], (DEC_BATCH, DEC_SEQ, D_MODEL)),
        "cache_sb_kv": nrm(ks[3], (DEPTH, n_pool, PAGE_SIZE, 2, H_SB, HEAD_DIM)),
        "cache_fox_kv": nrm(ks[4], (DEPTH, n_pool, PAGE_SIZE, 2, H_FOX, HEAD_DIM)),
        "cache_fox_logf": jax.nn.log_sigmoid(nrm(ks[5], (DEPTH, n_pool, PAGE_SIZE, H_FOX)) + 3.0),
        "cache_nsa_cmp_kv": nrm(ks[6], (DEPTH, n_pool, PAGE_SIZE, 2, HKV_NSA, HEAD_DIM)),
        "cache_nsa_sel_kv": nrm(ks[7], (DEPTH, n_pool, PAGE_SIZE, 2, HKV_NSA, HEAD_DIM)),
        "state_nsa_win_kv": nrm(ks[8], (DEPTH, DEC_BATCH, win_buf, 2, HKV_NSA, HEAD_DIM)),
        "state_ffn_conv": nrm(ks[9], (DEPTH, DEC_BATCH, CONV_W - 1, D_FF)),
        "page_table": page_table,
        "attn_norm": 1.0 + nrm(ks[10], (DEPTH, D_MODEL), 0.02),
        "w_in": nrm(ks[11], (DEPTH, D_MODEL, N_IN), D_MODEL ** -0.5),
        "fox_forget_bias": jax.random.uniform(ks[12], (DEPTH, H_FOX), F32, 1.0, 5.0),
        "fox_qk_gain": 1.0 + nrm(ks[13], (DEPTH, 2, HEAD_DIM), 0.02),
        "nsa_qk_gain": 1.0 + nrm(ks[14], (DEPTH, 4, HEAD_DIM), 0.02),
        "nsa_cmp_pool": (1.0 + nrm(ks[15], (DEPTH, 2, CMP_BLOCK, HKV_NSA), 0.1)) / CMP_BLOCK,
        "w_br_sb": nrm(ks[16], (DEPTH, H_SB * HEAD_DIM, D_MODEL), (H_SB * HEAD_DIM) ** -0.5),
        "w_br_fox": nrm(ks[17], (DEPTH, H_FOX * HEAD_DIM, D_MODEL), (H_FOX * HEAD_DIM) ** -0.5),
        "w_br_nsa": nrm(ks[18], (DEPTH, H_NSA * HEAD_DIM, D_MODEL), (H_NSA * HEAD_DIM) ** -0.5),
        "w_o": nrm(ks[19], (DEPTH, D_MODEL, D_MODEL), D_MODEL ** -0.5),
        "ffn_norm": 1.0 + nrm(ks[20], (DEPTH, D_MODEL), 0.02),
        "w_up": nrm(ks[21], (DEPTH, D_MODEL, 2 * D_FF), D_MODEL ** -0.5),
        "conv_w": nrm(ks[22], (DEPTH, CONV_W, D_FF), CONV_W ** -0.5),
        "conv_b": nrm(ks[23], (DEPTH, D_FF), 0.02),
        "w_down": nrm(ks[24], (DEPTH, D_FF, D_MODEL), D_FF ** -0.5),
    }


def reference(x_prompt, x_sample, cache_sb_kv, cache_fox_kv, cache_fox_logf, cache_nsa_cmp_kv,
              cache_nsa_sel_kv, state_nsa_win_kv, state_ffn_conv, page_table,
              attn_norm, w_in, fox_forget_bias, fox_qk_gain, nsa_qk_gain, nsa_cmp_pool,
              w_br_sb, w_br_fox, w_br_nsa, w_o, ffn_norm, w_up, conv_w, conv_b, w_down):
    n_pages = page_table.shape[1]
    past_len = n_pages * PAGE_SIZE
    pos_p = jnp.arange(x_prompt.shape[1], dtype=jnp.int32)
    pos_s = past_len + jnp.arange(x_sample.shape[1], dtype=jnp.int32)

    def paged(cache, l):
        g = cache[l, page_table]
        return g.reshape(g.shape[0], n_pages * PAGE_SIZE, *g.shape[3:])

    hp, hs = x_prompt, x_sample
    st_p, st_s = [], []
    for l in range(DEPTH):
        lp = dict(attn_norm=attn_norm[l], w_in=w_in[l], fox_forget_bias=fox_forget_bias[l],
                  fox_qk_gain=fox_qk_gain[l], nsa_qk_gain=nsa_qk_gain[l], nsa_cmp_pool=nsa_cmp_pool[l],
                  w_br_sb=w_br_sb[l], w_br_fox=w_br_fox[l], w_br_nsa=w_br_nsa[l], w_o=w_o[l],
                  ffn_norm=ffn_norm[l], w_up=w_up[l], conv_w=conv_w[l], conv_b=conv_b[l], w_down=w_down[l])
        hp, new_p = trunk_layer(hp, pos_p, lp, None)
        past = dict(sb_kv=paged(cache_sb_kv, l), fox_kv=paged(cache_fox_kv, l),
                    fox_logf=paged(cache_fox_logf, l), cmp_kv=paged(cache_nsa_cmp_kv, l),
                    sel_kv=paged(cache_nsa_sel_kv, l), win_kv=state_nsa_win_kv[l], conv=state_ffn_conv[l])
        hs, new_s = trunk_layer(hs, pos_s, lp, past)
        st_p.append(new_p)
        st_s.append(new_s)

    stk = lambda lst, i: jnp.stack([s[i] for s in lst], axis=0)
    sb_kv_p, sb_kv_s = stk(st_p, 0), stk(st_s, 0)
    fox_kv_p, fox_kv_s = stk(st_p, 1), stk(st_s, 1)
    fox_logf_p, fox_logf_s = stk(st_p, 2), stk(st_s, 2)
    cmp_kv_p, cmp_kv_s = stk(st_p, 3), stk(st_s, 3)
    sel_kv_p, sel_kv_s = stk(st_p, 4), stk(st_s, 4)
    win_p, win_s = stk(st_p, 5), stk(st_s, 5)
    conv_p, conv_s = stk(st_p, 6), stk(st_s, 6)
    return (hp, hs, sb_kv_p, sb_kv_s, fox_kv_p, fox_kv_s, fox_logf_p, fox_logf_s,
            cmp_kv_p, cmp_kv_s, sel_kv_p, sel_kv_s, win_p, win_s, conv_p, conv_s)
```

```python
import functools

import numpy as np
import jax
import jax.numpy as jnp
from jax import lax
from jax.experimental import pallas as pl
from jax.experimental.pallas import tpu as pltpu

F32 = jnp.float32
BF16 = jnp.bfloat16

HEAD_DIM = 64
H_SB = 4
H_FOX = 4
H_NSA = 8
HKV_NSA = 2
GQA_R = H_NSA // HKV_NSA
ROPE_DIM = HEAD_DIM // 4
ROPE_THETA = 500000.0
CMP_BLOCK = 32
SEL_BLOCK = 64
SEL_TOPK = 16
WINDOW = 512
CONV_W = 3
EPS = 1e-6
SCALE = HEAD_DIM ** -0.5
NEG_INF = -1e30
TINY = 1e-30
FORCE_SCORE = 1e6
PAGE = 128
LANES = 128
VMEM_LIMIT = 56 * 1024 * 1024

C_SBQ, C_SBKV, C_FOXQ, C_FOXK, C_FOXV = 0, 256, 768, 1024, 1280
C_NSAQ, C_CMP, C_SEL, C_WIN, C_MISC, C_END = 1536, 2048, 2304, 2560, 2816, 2944
N_MG = 3 * 1024


def _dot(a, b):
    return jnp.dot(a, b, preferred_element_type=F32)


def _dot_nt(a, b):
    return lax.dot_general(a, b, (((1,), (1,)), ((), ())), preferred_element_type=F32)


def _split2(x):
    hi = x.astype(BF16)
    lo = (x - hi.astype(F32)).astype(BF16)
    return hi, lo


def _split3(x):
    hi = x.astype(BF16)
    r = x - hi.astype(F32)
    mid = r.astype(BF16)
    lo = (r - mid.astype(F32)).astype(BF16)
    return hi, mid, lo


def _softplus(z):
    return jnp.maximum(z, 0.0) + jnp.log1p(jnp.exp(-jnp.abs(z)))


def _iota(shape, dim):
    return lax.broadcasted_iota(jnp.int32, shape, dim)


def _cparams(sem):
    return pltpu.CompilerParams(dimension_semantics=sem, vmem_limit_bytes=VMEM_LIMIT)


def _proj_kernel(x_ref, g_ref, w_ref, bd_ref, gains_ref, cos_ref, sa_ref, sb_ref, fb_ref,
                 sbq_ref, sbkv_ref, foxq_ref, foxkv_ref, nsaq_ref, cmp_ref, sel_ref, win_ref, misc_ref):
    x = x_ref[...]
    h = x * lax.rsqrt(jnp.mean(x * x, axis=-1, keepdims=True) + EPS) * g_ref[...]
    hb = h.astype(BF16)

    def seg(lo, n):
        return _dot(hb, w_ref[:, lo:lo + n])

    def hrms(z, gain):
        w = z.shape[1]
        hi, lo = _split2(z * z)
        bd = bd_ref[0:w, 0:w]
        ss = _dot(hi, bd) + _dot(lo, bd)
        return z * lax.rsqrt(ss * (1.0 / HEAD_DIM) + EPS) * gain

    def rope(z):
        return (z * cos_ref[...] + pltpu.roll(z, LANES - ROPE_DIM // 2, 1) * sa_ref[...]
                + pltpu.roll(z, ROPE_DIM // 2, 1) * sb_ref[...])

    sbq_ref[...] = (seg(C_SBQ, 256) * SCALE).astype(BF16)
    sbkv_ref[...] = seg(C_SBKV, 512)
    foxq_ref[...] = (hrms(seg(C_FOXQ, 256), gains_ref[0:1, :]) * SCALE).astype(BF16)
    foxkv_ref[:, 0:256] = hrms(seg(C_FOXK, 256), gains_ref[1:2, :])
    foxkv_ref[:, 256:512] = seg(C_FOXV, 256)
    for c in range(2):
        zq = hrms(seg(C_NSAQ + 256 * c, 256), gains_ref[2:3, :])
        for s in range(2):
            nsaq_ref[:, 256 * c + 128 * s:256 * c + 128 * (s + 1)] = (
                rope(zq[:, 128 * s:128 * (s + 1)]) * SCALE).astype(BF16)
    for i, (ref, col) in enumerate(((cmp_ref, C_CMP), (sel_ref, C_SEL), (win_ref, C_WIN))):
        ref[:, 0:128] = rope(hrms(seg(col, 128), gains_ref[3 + i:4 + i, 0:128]))
        ref[:, 128:256] = seg(col + 128, 128)
    zm = seg(C_MISC, 128)
    lane = _iota((1, LANES), 1)
    misc_ref[...] = jnp.where(lane < H_FOX, -_softplus(-(zm + fb_ref[...])), jax.nn.sigmoid(zm))


def _proj(x2, lw, cos_t, sa_t, sb_t, tm):
    n, d = x2.shape
    nt = cos_t.shape[0] // tm
    row = lambda i: (i, 0)
    const = lambda i: (0, 0)
    tab = pl.BlockSpec((tm, LANES), lambda i: (i % nt, 0))
    outs = [(256, BF16), (512, F32), (256, BF16), (512, F32), (512, BF16),
            (256, F32), (256, F32), (256, F32), (128, F32)]
    return pl.pallas_call(
        _proj_kernel,
        grid=(n // tm,),
        in_specs=[pl.BlockSpec((tm, d), row), pl.BlockSpec((1, d), const),
                  pl.BlockSpec((d, C_END), const), pl.BlockSpec((256, 256), const),
                  pl.BlockSpec((8, 256), const), tab, tab, tab, pl.BlockSpec((1, LANES), const)],
        out_specs=[pl.BlockSpec((tm, w), row) for w, _ in outs],
        out_shape=[jax.ShapeDtypeStruct((n, w), dt) for w, dt in outs],
        compiler_params=_cparams(("parallel",)),
        name="proj",
    )(x2, lw["attn_norm"], lw["w_a"], lw["bd"], lw["gains"], cos_t, sa_t, sb_t, lw["fbias"])


def _logf_kernel(x_ref, c_ref, o_ref):
    parts = _split3(x_ref[...])
    c = c_ref[...]
    o_ref[...] = _dot(parts[0], c) + _dot(parts[1], c) + _dot(parts[2], c)


def _logf_sums(lf_pages, cmat):
    p = lf_pages.shape[0]
    tp = 256 if p % 256 == 0 else p
    return pl.pallas_call(
        _logf_kernel,
        grid=(p // tp,),
        in_specs=[pl.BlockSpec((tp, 512), lambda i: (i, 0)), pl.BlockSpec((512, 1536), lambda i: (0, 0))],
        out_specs=pl.BlockSpec((tp, 1536), lambda i: (i, 0)),
        out_shape=jax.ShapeDtypeStruct((p, 1536), F32),
        compiler_params=_cparams(("parallel",)),
        name="logf",
    )(lf_pages, cmat)


def _head_layout(nsa, n):
    slab, sub = divmod(n, 2)
    if nsa:
        return 128 * slab, 0, 128, sub
    return 128 * slab, 128 * slab, 256 + 128 * slab, sub


def _attn_kernel(mode, nsa, tq, tk, n_s, qi_ref, kb_ref, *refs):
    refs = list(refs)
    q_ref, kv_ref = refs[0], refs[1]
    pos = 2
    tri_ref = rt_ref = sm_ref = None
    if mode == "sb":
        tri_ref = refs[pos]; pos += 1
    if mode == "fox":
        rt_ref = refs[pos]; pos += 1
    if mode == "sel":
        sm_ref = refs[pos]; pos += 1
    o_ref, acc_ref, m_ref, l_ref = refs[pos:pos + 4]
    cf_ref = refs[pos + 4] if mode == "fox" else None
    nh = H_NSA if nsa else H_SB
    npg = tk // PAGE

    p_id = pl.program_id(1)
    qi = qi_ref[p_id]
    kb = kb_ref[p_id]
    first = kb == qi
    last = (kb == jnp.maximum(qi - WINDOW // tk, 0)) if mode == "win" else (kb == 0)

    @pl.when(first)
    def _init():
        acc_ref[...] = jnp.zeros_like(acc_ref)
        l_ref[...] = jnp.zeros_like(l_ref)
        m_ref[...] = jnp.zeros_like(m_ref) if mode == "sb" else jnp.full_like(m_ref, NEG_INF)
        if mode == "fox":
            cf_ref[...] = jnp.zeros_like(cf_ref)

    d = (qi * tq + _iota((tq, tk), 0)) - (kb * tk + _iota((tq, tk), 1))
    if mode == "sb":
        mask = d > 0
    elif mode == "win":
        mask = (d >= 0) & (d < WINDOW)
    else:
        mask = d >= 0
    lane_lo = _iota((1, LANES), 1) < HEAD_DIM

    masks = [mask, mask]
    if mode == "sel":
        sm = sm_ref[...]
        spb = tk // SEL_BLOCK
        for g in range(2):
            e = (_iota((2 * n_s, tk), 0) == g * n_s + kb * spb + _iota((2 * n_s, tk), 1) // SEL_BLOCK)
            masks[g] = mask & (_dot(sm, e.astype(BF16)) > 0.5)

    kw = 128 if nsa else 256
    kb16 = kv_ref[:, 0:kw].astype(BF16)
    vb16 = kv_ref[:, kw:2 * kw].astype(BF16)

    for n in range(nh):
        qo, ko, vo, sub = _head_layout(nsa, n)
        vo -= kw
        hm = lane_lo if sub == 0 else jnp.logical_not(lane_lo)
        qh = jnp.where(hm, q_ref[:, qo:qo + 128], jnp.zeros((), BF16))
        s = _dot_nt(qh, kb16[:, ko:ko + 128])
        vh = vb16[:, vo:vo + 128]
        if mode == "sb":
            sp = _softplus(s)
            lk = jnp.where(mask, -sp, 0.0)
            hi, lo = _split2(lk)
            tri = tri_ref[...]
            al = _dot(hi, tri) + _dot(lo, tri)
            carry = m_ref[n]
            a = jnp.where(mask, jnp.exp(s - sp + al + carry), 0.0)
            acc_ref[n] += _dot(a.astype(BF16), vh)
            m_ref[n] = carry + al[:, 0:1] + lk[:, 0:1]
        else:
            if mode == "fox":
                cf = cf_ref[n]
                pieces = []
                later = cf
                for pg in reversed(range(npg)):
                    pieces.append(rt_ref[0, pg:pg + 1, 128 * n:128 * (n + 1)] + later)
                    later = later + rt_ref[0, pg:pg + 1, 1024 + 128 * n:1024 + 128 * (n + 1)]
                s = s + jnp.concatenate(pieces[::-1], axis=1)
                cf_ref[n] = later
            s = jnp.where(masks[sub], s, NEG_INF)
            m_old = m_ref[n]
            m_new = jnp.maximum(m_old, jnp.max(s, axis=-1, keepdims=True))
            p = jnp.exp(s - m_new)
            alpha = jnp.exp(m_old - m_new)
            l_ref[n] = alpha * l_ref[n] + jnp.sum(p, axis=-1, keepdims=True)
            acc_ref[n] = alpha * acc_ref[n] + _dot(p.astype(BF16), vh)
            m_ref[n] = m_new

    @pl.when(last)
    def _fin():
        for slab in range(nh // 2):
            if mode == "sb":
                o0, o1 = acc_ref[2 * slab], acc_ref[2 * slab + 1]
            else:
                o0 = acc_ref[2 * slab] / l_ref[2 * slab]
                o1 = acc_ref[2 * slab + 1] / l_ref[2 * slab + 1]
            o_ref[:, 128 * slab:128 * (slab + 1)] = jnp.where(lane_lo, o0, o1)


def _pairs(nq, mode, tk):
    qi, kb = [], []
    for i in range(nq):
        lo = max(i - WINDOW // tk, 0) if mode == "win" else 0
        for k in range(i, lo - 1, -1):
            qi.append(i)
            kb.append(k)
    return jnp.asarray(qi, jnp.int32), jnp.asarray(kb, jnp.int32)


def _attn(mode, q, kv, b, t, extra=None, tq=256, tk=256):
    nsa = mode in ("sel", "win")
    n, wq = q.shape
    wkv = kv.shape[1]
    nq = t // tq
    n_s = t // SEL_BLOCK
    qi_tab, kb_tab = _pairs(nq, mode, tk)
    qmap = lambda bi, p, qi, kb: (bi * nq + qi[p], 0)
    kmap = lambda bi, p, qi, kb: (bi * (t // tk) + kb[p], 0)
    in_specs = [pl.BlockSpec((tq, wq), qmap), pl.BlockSpec((tk, wkv), kmap)]
    args = [q, kv]
    if mode == "sb":
        in_specs.append(pl.BlockSpec((tk, tk), lambda bi, p, qi, kb: (0, 0)))
        args.append(extra)
    elif mode == "fox":
        npg = tk // PAGE
        in_specs.append(pl.BlockSpec((1, npg, 1536), lambda bi, p, qi, kb: (bi * (t // tk) + kb[p], 0, 0)))
        args.append(extra.reshape(-1, npg, 1536))
    elif mode == "sel":
        in_specs.append(pl.BlockSpec((tq, 2 * n_s), qmap))
        args.append(extra)
    nh = H_NSA if nsa else H_SB
    scratch = [pltpu.VMEM((nh, tq, 128), F32), pltpu.VMEM((nh, tq, 1), F32), pltpu.VMEM((nh, tq, 1), F32)]
    if mode == "fox":
        scratch.append(pltpu.VMEM((nh, 1, 128), F32))
    return pl.pallas_call(
        functools.partial(_attn_kernel, mode, nsa, tq, tk, n_s),
        grid_spec=pltpu.PrefetchScalarGridSpec(
            num_scalar_prefetch=2, grid=(b, int(qi_tab.shape[0])),
            in_specs=in_specs, out_specs=pl.BlockSpec((tq, wq), qmap), scratch_shapes=scratch),
        out_shape=jax.ShapeDtypeStruct((n, wq), F32),
        compiler_params=_cparams(("parallel", "arbitrary")),
        name="attn_" + mode,
    )(qi_tab, kb_tab, *args)


def _pool_blocks(x, pwe_ref, pwo_ref):
    n = x.shape[0] // SEL_BLOCK
    x3 = x.reshape(n, SEL_BLOCK, 256)
    return (jnp.sum(x3 * pwe_ref[...][None], axis=1), jnp.sum(x3 * pwo_ref[...][None], axis=1))


def _rank_select(sc, n_blk, k_sel):
    j = _iota((n_blk, 1), 0)
    rank = jnp.zeros(sc.shape, F32)
    for i in range(n_blk):
        ri = sc[i:i + 1, :]
        beats = (ri > sc) | ((ri == sc) & (j > i))
        rank = rank + beats.astype(F32)
    return (rank < k_sel).astype(F32)


def _cmp_p_kernel(tq, t, q_ref, ckv_ref, pwe_ref, pwo_ref, o_ref, sm_ref, kc_ref):
    n_s = t // SEL_BLOCK
    n_c = 2 * n_s
    qi = pl.program_id(1)

    @pl.when(qi == 0)
    def _pool():
        ev, od = _pool_blocks(ckv_ref[...], pwe_ref, pwo_ref)
        kc_ref[0:n_s, :] = ev
        kc_ref[n_s:n_c, :] = od

    kcb = kc_ref[:, 0:128].astype(BF16)
    vcb = kc_ref[:, 128:256].astype(BF16)

    def c_end(i):
        blk = jnp.where(i < n_s, 2 * i, 2 * (i - n_s) + 1)
        return (blk + 1) * CMP_BLOCK - 1

    cmask = c_end(_iota((1, n_c), 1)) <= qi * tq + _iota((tq, 1), 0)
    t_row = qi * tq + _iota((1, tq), 1)
    cmask_t = c_end(_iota((n_c, 1), 0)) <= t_row
    lane_lo = _iota((1, LANES), 1) < HEAD_DIM
    imp = [jnp.zeros((n_s, tq), F32), jnp.zeros((n_s, tq), F32)]
    outs = {}
    for n in range(H_NSA):
        r, g = divmod(n, 2)
        hm = lane_lo if g == 0 else jnp.logical_not(lane_lo)
        qh = jnp.where(hm, q_ref[:, 128 * r:128 * (r + 1)], jnp.zeros((), BF16))
        s = jnp.where(cmask, _dot_nt(qh, kcb), NEG_INF)
        e = jnp.where(cmask, jnp.exp(s - jnp.max(s, axis=-1, keepdims=True)), 0.0)
        p = e / jnp.maximum(jnp.sum(e, axis=-1, keepdims=True), TINY)
        outs[(r, g)] = _dot(p.astype(BF16), vcb)
        st = jnp.where(cmask_t, _dot_nt(kcb, qh), NEG_INF)
        et = jnp.where(cmask_t, jnp.exp(st - jnp.max(st, axis=0, keepdims=True)), 0.0)
        pt = et / jnp.maximum(jnp.sum(et, axis=0, keepdims=True), TINY)
        imp[g] = imp[g] + (pt[0:n_s, :] + pt[n_s:n_c, :])
    for r in range(GQA_R):
        o_ref[:, 128 * r:128 * (r + 1)] = jnp.where(lane_lo, outs[(r, 0)], outs[(r, 1)])

    j = _iota((n_s, 1), 0)
    cur = t_row // SEL_BLOCK
    forced = (j == 0) | (j == cur) | (j == cur - 1)
    valid = j * SEL_BLOCK <= t_row
    sel = []
    for g in range(2):
        sc = jnp.where(forced, FORCE_SCORE, jnp.where(valid, imp[g], -1.0))
        sel.append(_rank_select(sc, n_s, min(SEL_TOPK, n_s)))
    sm_ref[...] = jnp.concatenate(sel, axis=0).T.astype(BF16)


def _cmp_p(nsaq, ckv, lw, b, t, tq=256):
    n = nsaq.shape[0]
    nq = t // tq
    n_s = t // SEL_BLOCK
    rowmap = lambda bi, qi: (bi * nq + qi, 0)
    const = lambda bi, qi: (0, 0)
    return pl.pallas_call(
        functools.partial(_cmp_p_kernel, tq, t),
        grid=(b, nq),
        in_specs=[pl.BlockSpec((tq, 512), rowmap), pl.BlockSpec((t, 256), lambda bi, qi: (bi, 0)),
                  pl.BlockSpec((SEL_BLOCK, 256), const), pl.BlockSpec((SEL_BLOCK, 256), const)],
        out_specs=[pl.BlockSpec((tq, 512), rowmap), pl.BlockSpec((tq, 2 * n_s), rowmap)],
        out_shape=[jax.ShapeDtypeStruct((n, 512), F32), jax.ShapeDtypeStruct((n, 2 * n_s), BF16)],
        scratch_shapes=[pltpu.VMEM((2 * n_s, 256), F32)],
        compiler_params=_cparams(("parallel", "arbitrary")),
        name="cmp_p",
    )(nsaq, ckv, lw["pwe"], lw["pwo"])


G_PAGES = 8


def _dec_kernel(mode, n_pages, n_sp, pt_ref, *refs):
    refs = list(refs)
    nsa = mode == "sel"
    g_n = G_PAGES
    q_ref, new_ref = refs[0], refs[1]
    pages = refs[2:2 + g_n]
    pos = 2 + g_n
    tri_ref = sm_ref = rtn_ref = None
    rts = None
    if mode == "sb":
        tri_ref = refs[pos]; pos += 1
    if mode == "fox":
        rtn_ref = refs[pos]; pos += 1
        rts = refs[pos:pos + g_n]; pos += g_n
    if mode == "sel":
        sm_ref = refs[pos]; pos += 1
    o_ref, qbd_ref, acc_ref, m_ref, l_ref = refs[pos:pos + 5]
    cf_ref = refs[pos + 5] if mode == "fox" else None

    t_s = q_ref.shape[0]
    nh = H_NSA if nsa else H_SB
    rows = nh * t_s
    kw = 128 if nsa else 256
    step = pl.program_id(1)
    n_steps = pl.num_programs(1)
    lane_lo = _iota((1, LANES), 1) < HEAD_DIM
    t_of_row = _iota((rows, 1), 0) % t_s

    def head_lanes(n, width):
        ln = _iota((1, width), 1)
        if nsa:
            return None
        return (ln >= HEAD_DIM * n) & (ln < HEAD_DIM * (n + 1))

    def block(k32, v32, mask, bias):
        kb16 = k32.astype(BF16)
        vb16 = v32.astype(BF16)
        s = _dot_nt(qbd_ref[...], kb16)
        if mode == "sb":
            sp = _softplus(s)
            lk = -sp if mask is None else jnp.where(mask, -sp, 0.0)
            hi, lo = _split2(lk)
            tri = tri_ref[...]
            al = _dot(hi, tri) + _dot(lo, tri)
            carry = m_ref[...]
            a = jnp.exp(s - sp + al + carry)
            if mask is not None:
                a = jnp.where(mask, a, 0.0)
            acc_ref[...] += _dot(a.astype(BF16), vb16)
            m_ref[...] = carry + al[:, 0:1] + lk[:, 0:1]
        else:
            if bias is not None:
                s = s + bias
            if mask is not None:
                s = jnp.where(mask, s, NEG_INF)
            m_old = m_ref[...]
            m_new = jnp.maximum(m_old, jnp.max(s, axis=-1, keepdims=True))
            p = jnp.exp(s - m_new)
            alpha = jnp.exp(m_old - m_new)
            l_ref[...] = alpha * l_ref[...] + jnp.sum(p, axis=-1, keepdims=True)
            acc_ref[...] = alpha * acc_ref[...] + _dot(p.astype(BF16), vb16)
            m_ref[...] = m_new

    def rows_of(vecs):
        return jnp.concatenate([jnp.broadcast_to(v, (t_s, 128)) for v in vecs], axis=0)

    @pl.when(step == 0)
    def _first():
        q = q_ref[...]
        if nsa:
            parts = []
            for g in range(HKV_NSA):
                hm = lane_lo if g == 0 else jnp.logical_not(lane_lo)
                for r in range(GQA_R):
                    parts.append(jnp.where(hm, q[:, 128 * r:128 * (r + 1)], jnp.zeros((), BF16)))
        else:
            parts = [jnp.where(head_lanes(n, 256), q, jnp.zeros((), BF16)) for n in range(nh)]
        qbd_ref[...] = jnp.concatenate(parts, axis=0)
        acc_ref[...] = jnp.zeros_like(acc_ref)
        l_ref[...] = jnp.zeros_like(l_ref)
        m_ref[...] = jnp.zeros_like(m_ref) if mode == "sb" else jnp.full_like(m_ref, NEG_INF)
        key = _iota((rows, PAGE), 1)
        mask = (key < t_of_row) if mode == "sb" else (key <= t_of_row)
        bias = None
        if mode == "fox":
            cf_ref[...] = jnp.zeros_like(cf_ref)
            bias = -rows_of([rtn_ref[0, :, 512 + 128 * n:512 + 128 * (n + 1)] for n in range(nh)])
        block(new_ref[0, :, 0:kw], new_ref[0, :, kw:2 * kw], mask, bias)

    @pl.when(step > 0)
    def _pages():
        b = pl.program_id(0)
        for g in reversed(range(g_n)):
            pref = pages[g]
            bias = None
            mask = None
            if mode == "fox":
                rt = rts[g]
                cf = cf_ref[...]
                bias = rows_of([rt[0, :, 128 * n:128 * (n + 1)] for n in range(nh)]) + cf
                cf_ref[...] = cf + rows_of([rt[0, :, 1024 + 128 * n:1024 + 128 * (n + 1)] for n in range(nh)])
            if mode == "sel":
                page_i = n_pages - g_n * step + g
                blk = page_i * (PAGE // SEL_BLOCK) + _iota((2 * n_sp, PAGE), 1) // SEL_BLOCK
                e = (_iota((2 * n_sp, PAGE), 0) % n_sp) == blk
                grp = _iota((rows, 2 * n_sp), 0) // (GQA_R * t_s) == _iota((rows, 2 * n_sp), 1) // n_sp
                smr = jnp.concatenate([sm_ref[...]] * nh, axis=0)
                smr = jnp.where(grp, smr, jnp.zeros((), BF16))
                mask = _dot(smr, e.astype(BF16)) > 0.5
            block(pref[0, :, 0:kw], pref[0, :, kw:2 * kw], mask, bias)

    @pl.when(step == n_steps - 1)
    def _fin():
        o = acc_ref[...] if mode == "sb" else acc_ref[...] / l_ref[...]
        if nsa:
            for r in range(GQA_R):
                o0 = o[t_s * r:t_s * (r + 1), :]
                o1 = o[t_s * (GQA_R + r):t_s * (GQA_R + r + 1), :]
                o_ref[:, 128 * r:128 * (r + 1)] = jnp.where(lane_lo, o0, o1)
        else:
            out = jnp.zeros((t_s, 256), F32)
            for n in range(nh):
                out = out + jnp.where(head_lanes(n, 256), o[t_s * n:t_s * (n + 1), :], 0.0)
            o_ref[...] = out


def _dec(mode, q, newkv, cache, page_tab, extra=None, rt_new=None, rt_cache=None):
    nsa = mode == "sel"
    bs, n_pages = page_tab.shape
    t_s = q.shape[0] // bs
    wq = q.shape[1]
    wkv = cache.shape[2]
    g_n = G_PAGES
    n_steps = 1 + n_pages // g_n
    n_sp = n_pages * PAGE // SEL_BLOCK
    nh = H_NSA if nsa else H_SB
    rows = nh * t_s

    def page_map(g):
        def f(b, s, pt):
            return (pt[b * n_pages + n_pages - g_n * jnp.maximum(s, 1) + g], 0, 0)
        return f

    in_specs = [pl.BlockSpec((t_s, wq), lambda b, s, pt: (b, 0)),
                pl.BlockSpec((1, PAGE, wkv), lambda b, s, pt: (b, 0, 0))]
    in_specs += [pl.BlockSpec((1, PAGE, wkv), page_map(g)) for g in range(g_n)]
    args = [q, newkv] + [cache] * g_n
    if mode == "sb":
        in_specs.append(pl.BlockSpec((PAGE, PAGE), lambda b, s, pt: (0, 0)))
        args.append(extra)
    if mode == "fox":
        in_specs.append(pl.BlockSpec((1, 1, 1536), lambda b, s, pt: (b, 0, 0)))
        args.append(rt_new)
        in_specs += [pl.BlockSpec((1, 1, 1536), page_map(g)) for g in range(g_n)]
        args += [rt_cache] * g_n
    if mode == "sel":
        in_specs.append(pl.BlockSpec((t_s, 2 * n_sp), lambda b, s, pt: (b, 0)))
        args.append(extra)
    scratch = [pltpu.VMEM((rows, 128 if nsa else 256), BF16), pltpu.VMEM((rows, 128 if nsa else 256), F32),
               pltpu.VMEM((rows, 1), F32), pltpu.VMEM((rows, 1), F32)]
    if mode == "fox":
        scratch.append(pltpu.VMEM((rows, 128), F32))
    return pl.pallas_call(
        functools.partial(_dec_kernel, mode, n_pages, n_sp),
        grid_spec=pltpu.PrefetchScalarGridSpec(
            num_scalar_prefetch=1, grid=(bs, n_steps), in_specs=in_specs,
            out_specs=pl.BlockSpec((t_s, wq), lambda b, s, pt: (b, 0)), scratch_shapes=scratch),
        out_shape=jax.ShapeDtypeStruct((bs * t_s, wq), F32),
        compiler_params=_cparams(("parallel", "arbitrary")),
        name="dec_" + mode,
    )(page_tab.reshape(-1), *args)


def _cmp_s_kernel(n_pages, pt_ref, q_ref, *refs):
    g_n = G_PAGES
    pages = refs[0:g_n]
    pwe_ref, pwo_ref, o_ref, sm_ref, kc_ref = refs[g_n:g_n + 5]
    t_s = q_ref.shape[0]
    n_sp = n_pages * PAGE // SEL_BLOCK
    n_c = 2 * n_sp
    per = g_n * PAGE // SEL_BLOCK
    step = pl.program_id(1)
    x = jnp.concatenate([p[0] for p in pages], axis=0)
    ev, od = _pool_blocks(x, pwe_ref, pwo_ref)
    off = pl.multiple_of(step * per, per)
    kc_ref[pl.ds(off, per), :] = ev
    kc_ref[pl.ds(n_sp + off, per), :] = od

    @pl.when(step == pl.num_programs(1) - 1)
    def _fin():
        kcb = kc_ref[:, 0:128].astype(BF16)
        vcb = kc_ref[:, 128:256].astype(BF16)
        lane_lo = _iota((1, LANES), 1) < HEAD_DIM
        q = q_ref[...]
        imps = []
        outs = []
        for g in range(HKV_NSA):
            hm = lane_lo if g == 0 else jnp.logical_not(lane_lo)
            qg = jnp.concatenate([jnp.where(hm, q[:, 128 * r:128 * (r + 1)], jnp.zeros((), BF16))
                                  for r in range(GQA_R)], axis=0)
            s = _dot_nt(qg, kcb)
            e = jnp.exp(s - jnp.max(s, axis=-1, keepdims=True))
            p = e / jnp.maximum(jnp.sum(e, axis=-1, keepdims=True), TINY)
            outs.append(_dot(p.astype(BF16), vcb))
            imp = jnp.zeros((t_s, n_sp), F32)
            for r in range(GQA_R):
                pr = p[t_s * r:t_s * (r + 1), :]
                imp = imp + (pr[:, 0:n_sp] + pr[:, n_sp:n_c])
            imps.append(imp)
        for r in range(GQA_R):
            o_ref[:, 128 * r:128 * (r + 1)] = jnp.where(
                lane_lo, outs[0][t_s * r:t_s * (r + 1), :], outs[1][t_s * r:t_s * (r + 1), :])
        imp = jnp.concatenate(imps, axis=0)
        jb = _iota((2 * t_s, n_sp), 1)
        sc = jnp.where((jb == 0) | (jb == n_sp - 1), FORCE_SCORE, imp)
        chosen = jnp.zeros(sc.shape, F32)
        for _ in range(min(SEL_TOPK - 1, n_sp)):
            idx = jnp.argmax(sc, axis=1).astype(jnp.int32)[:, None]
            hit = jb == idx
            chosen = jnp.where(hit, 1.0, chosen)
            sc = jnp.where(hit, -jnp.inf, sc)
        sm_ref[...] = jnp.concatenate([chosen[0:t_s], chosen[t_s:2 * t_s]], axis=1).astype(BF16)


def _cmp_s(nsaq, cache, page_tab, lw):
    bs, n_pages = page_tab.shape
    t_s = nsaq.shape[0] // bs
    g_n = G_PAGES
    n_sp = n_pages * PAGE // SEL_BLOCK

    def page_map(g):
        return lambda b, s, pt: (pt[b * n_pages + g_n * s + g], 0, 0)

    const = lambda b, s, pt: (0, 0)
    return pl.pallas_call(
        functools.partial(_cmp_s_kernel, n_pages),
        grid_spec=pltpu.PrefetchScalarGridSpec(
            num_scalar_prefetch=1, grid=(bs, n_pages // g_n),
            in_specs=[pl.BlockSpec((t_s, 512), lambda b, s, pt: (b, 0))]
                     + [pl.BlockSpec((1, PAGE, 256), page_map(g)) for g in range(g_n)]
                     + [pl.BlockSpec((SEL_BLOCK, 256), const), pl.BlockSpec((SEL_BLOCK, 256), const)],
            out_specs=[pl.BlockSpec((t_s, 512), lambda b, s, pt: (b, 0)),
                       pl.BlockSpec((t_s, 2 * n_sp), lambda b, s, pt: (b, 0))],
            scratch_shapes=[pltpu.VMEM((2 * n_sp, 256), F32)]),
        out_shape=[jax.ShapeDtypeStruct((bs * t_s, 512), F32), jax.ShapeDtypeStruct((bs * t_s, 2 * n_sp), BF16)],
        compiler_params=_cparams(("parallel", "arbitrary")),
        name="cmp_s",
    )(page_tab.reshape(-1), nsaq, *([cache] * g_n), lw["pwe"], lw["pwo"])


def _win_s_kernel(q_ref, st_ref, new_ref, o_ref):
    t_s = q_ref.shape[0]
    n_w = st_ref.shape[1]
    rows = H_NSA * t_s
    lane_lo = _iota((1, LANES), 1) < HEAD_DIM
    q = q_ref[...]
    parts = []
    for g in range(HKV_NSA):
        hm = lane_lo if g == 0 else jnp.logical_not(lane_lo)
        for r in range(GQA_R):
            parts.append(jnp.where(hm, q[:, 128 * r:128 * (r + 1)], jnp.zeros((), BF16)))
    qbd = jnp.concatenate(parts, axis=0)
    t_of_row = _iota((rows, 1), 0) % t_s
    jp = _iota((rows, n_w), 1)
    mask_p = (jp > t_of_row + (n_w - WINDOW)) & (jp <= t_of_row + n_w)
    jn = _iota((rows, PAGE), 1)
    mask_n = jn <= t_of_row
    s_p = jnp.where(mask_p, _dot_nt(qbd, st_ref[0, :, 0:128].astype(BF16)), NEG_INF)
    s_n = jnp.where(mask_n, _dot_nt(qbd, new_ref[0, :, 0:128].astype(BF16)), NEG_INF)
    mx = jnp.maximum(jnp.max(s_p, axis=-1, keepdims=True), jnp.max(s_n, axis=-1, keepdims=True))
    p_p = jnp.exp(s_p - mx)
    p_n = jnp.exp(s_n - mx)
    den = jnp.sum(p_p, axis=-1, keepdims=True) + jnp.sum(p_n, axis=-1, keepdims=True)
    o = (_dot(p_p.astype(BF16), st_ref[0, :, 128:256].astype(BF16))
         + _dot(p_n.astype(BF16), new_ref[0, :, 128:256].astype(BF16))) / den
    for r in range(GQA_R):
        o_ref[:, 128 * r:128 * (r + 1)] = jnp.where(
            lane_lo, o[t_s * r:t_s * (r + 1), :], o[t_s * (GQA_R + r):t_s * (GQA_R + r + 1), :])


def _win_s(nsaq, state, newkv, bs):
    t_s = nsaq.shape[0] // bs
    n_w = state.shape[1]
    return pl.pallas_call(
        _win_s_kernel,
        grid=(bs,),
        in_specs=[pl.BlockSpec((t_s, 512), lambda b: (b, 0)), pl.BlockSpec((1, n_w, 256), lambda b: (b, 0, 0)),
                  pl.BlockSpec((1, PAGE, 256), lambda b: (b, 0, 0))],
        out_specs=pl.BlockSpec((t_s, 512), lambda b: (b, 0)),
        out_shape=jax.ShapeDtypeStruct((bs * t_s, 512), F32),
        compiler_params=_cparams(("parallel",)),
        name="win_s",
    )(nsaq, state, newkv)


def _merge_kernel(x_ref, g_ref, osb_ref, ofox_ref, ocmp_ref, osel_ref, owin_ref, misc_ref,
                  wmg_ref, wsb_ref, wfox_ref, wnsa_ref, wo_ref, ex_ref, y_ref):
    x = x_ref[...]
    h = x * lax.rsqrt(jnp.mean(x * x, axis=-1, keepdims=True) + EPS) * g_ref[...]
    hb = h.astype(BF16)
    mhi, mlo = _split2(misc_ref[...])

    def gate(i):
        ex = ex_ref[:, 512 * i:512 * (i + 1)]
        return _dot(mhi, ex) + _dot(mlo, ex)

    o_nsa = gate(0) * ocmp_ref[...] + gate(1) * osel_ref[...] + gate(2) * owin_ref[...]
    d = x.shape[1]

    def mg(i):
        return jax.nn.sigmoid(_dot(hb, wmg_ref[:, d * i:d * (i + 1)]))

    mixed = (mg(0) * _dot(osb_ref[...].astype(BF16), wsb_ref[...])
             + mg(1) * _dot(ofox_ref[...].astype(BF16), wfox_ref[...])
             + mg(2) * _dot(o_nsa.astype(BF16), wnsa_ref[...]))
    y_ref[...] = x + _dot(mixed.astype(BF16), wo_ref[...])


def _merge(x2, o_sb, o_fox, o_cmp, o_sel, o_win, misc, lw, tm):
    n, d = x2.shape
    row = lambda i: (i, 0)
    const = lambda i: (0, 0)
    return pl.pallas_call(
        _merge_kernel,
        grid=(n // tm,),
        in_specs=[pl.BlockSpec((tm, d), row), pl.BlockSpec((1, d), const),
                  pl.BlockSpec((tm, 256), row), pl.BlockSpec((tm, 256), row),
                  pl.BlockSpec((tm, 512), row), pl.BlockSpec((tm, 512), row), pl.BlockSpec((tm, 512), row),
                  pl.BlockSpec((tm, 128), row),
                  pl.BlockSpec((d, N_MG), const), pl.BlockSpec((256, d), const), pl.BlockSpec((256, d), const),
                  pl.BlockSpec((512, d), const), pl.BlockSpec((d, d), const), pl.BlockSpec((128, 1536), const)],
        out_specs=pl.BlockSpec((tm, d), row),
        out_shape=jax.ShapeDtypeStruct((n, d), F32),
        compiler_params=_cparams(("parallel",)),
        name="merge",
    )(x2, lw["attn_norm"], o_sb, o_fox, o_cmp, o_sel, o_win, misc,
      lw["w_mg"], lw["w_br_sb"], lw["w_br_fox"], lw["w_br_nsa"], lw["w_o"], lw["ex"])


FF_CHUNK = 768


def _ffn_kernel(x_ref, g_ref, hist_ref, wup_ref, cw_ref, cb_ref, wdn_ref, y_ref, cs_ref, prev_ref):
    tm = x_ref.shape[0]
    d_ff = wdn_ref.shape[0]
    ti = pl.program_id(1)

    @pl.when(ti == 0)
    def _hist():
        prev_ref[...] = hist_ref[0]

    x = x_ref[...]
    h = x * lax.rsqrt(jnp.mean(x * x, axis=-1, keepdims=True) + EPS) * g_ref[...]
    hb = h.astype(BF16)
    rowi = _iota((tm, 1), 0)
    y = x
    for c in range(0, d_ff, FF_CHUNK):
        gate = _dot(hb, wup_ref[:, c:c + FF_CHUNK])
        up = _dot(hb, wup_ref[:, d_ff + c:d_ff + c + FF_CHUNK])
        p6 = prev_ref[6:7, c:c + FF_CHUNK]
        p7 = prev_ref[7:8, c:c + FF_CHUNK]
        g1 = jnp.where(rowi == 0, p7, pltpu.roll(gate, 1, 0))
        g2 = jnp.where(rowi == 0, p6, jnp.where(rowi == 1, p7, pltpu.roll(gate, 2, 0)))
        conv = (cb_ref[:, c:c + FF_CHUNK] + g2 * cw_ref[0:1, c:c + FF_CHUNK]
                + g1 * cw_ref[1:2, c:c + FF_CHUNK] + gate * cw_ref[2:3, c:c + FF_CHUNK])
        act = conv * jax.nn.sigmoid(conv) * up
        y = y + _dot(act.astype(BF16), wdn_ref[c:c + FF_CHUNK, :])
        tail = gate[tm - 8:tm, :]
        prev_ref[:, c:c + FF_CHUNK] = tail
        cs_ref[0, :, c:c + FF_CHUNK] = tail
    y_ref[...] = y


def _ffn(x2, hist, lw, nb, tm):
    n, d = x2.shape
    nt = n // nb // tm
    d_ff = lw["w_down"].shape[0]
    row = lambda b, i: (b * nt + i, 0)
    const = lambda b, i: (0, 0)
    hmap = (lambda b, i: (b, 0, 0)) if hist.shape[0] == nb else (lambda b, i: (0, 0, 0))
    once = dict(pipeline_mode=pl.Buffered(1))
    return pl.pallas_call(
        _ffn_kernel,
        grid=(nb, nt),
        in_specs=[pl.BlockSpec((tm, d), row), pl.BlockSpec((1, d), const), pl.BlockSpec((1, 8, d_ff), hmap),
                  pl.BlockSpec((d, 2 * d_ff), const, **once), pl.BlockSpec((CONV_W, d_ff), const),
                  pl.BlockSpec((1, d_ff), const), pl.BlockSpec((d_ff, d), const, **once)],
        out_specs=[pl.BlockSpec((tm, d), row), pl.BlockSpec((1, 8, d_ff), lambda b, i: (b, 0, 0))],
        out_shape=[jax.ShapeDtypeStruct((n, d), F32), jax.ShapeDtypeStruct((nb, 8, d_ff), F32)],
        scratch_shapes=[pltpu.VMEM((8, d_ff), F32)],
        compiler_params=_cparams(("parallel", "arbitrary")),
        name="ffn",
    )(x2, lw["ffn_norm"], hist, lw["w_up"], lw["conv_w"], lw["conv_b"], lw["w_down"])


def _consts():
    bd = (np.arange(256)[:, None] // HEAD_DIM == np.arange(256)[None, :] // HEAD_DIM)
    tri256 = np.arange(256)[:, None] > np.arange(256)[None, :]
    tri128 = np.arange(PAGE)[:, None] > np.arange(PAGE)[None, :]
    src_p, src_h = np.arange(512) // 4, np.arange(512) % 4
    dst_h, dst_p = np.arange(512) // 128, np.arange(512) % 128
    same = src_h[:, None] == dst_h[None, :]
    c_r = same & (src_p[:, None] > dst_p[None, :])
    c_w = same & (src_p[:, None] <= dst_p[None, :])
    cmat = np.concatenate([c_r, c_w, same], axis=1)
    ex = np.zeros((128, 1536), np.float32)
    for br in range(3):
        for g in range(HKV_NSA):
            for r in range(GQA_R):
                m = 2 * r + g
                ex[H_FOX + br * 8 + g * GQA_R + r, 512 * br + 64 * m:512 * br + 64 * (m + 1)] = 1.0
    as_bf = lambda a: jnp.asarray(a.astype(np.float32), BF16)
    return dict(bd=as_bf(bd), tri256=as_bf(tri256), tri128=as_bf(tri128), cmat=as_bf(cmat), ex=as_bf(ex))


def _layer_weights(l, cst, attn_norm, w_in, fox_forget_bias, fox_qk_gain, nsa_qk_gain, nsa_cmp_pool,
                   w_br_sb, w_br_fox, w_br_nsa, w_o, ffn_norm, w_up, conv_w, conv_b, w_down):
    w = w_in[l]
    d = w.shape[0]
    o_foxf = 6 * 256
    o_nsaq = o_foxf + H_FOX
    o_kv = o_nsaq + H_NSA * HEAD_DIM
    o_gate = o_kv + 6 * 128
    o_mg = o_gate + 3 * H_NSA
    head_order = [(m % 2) * GQA_R + m // 2 for m in range(H_NSA)]
    nsaq_cols = jnp.concatenate([w[:, o_nsaq + 64 * hh:o_nsaq + 64 * (hh + 1)] for hh in head_order], axis=1)
    misc = jnp.concatenate([w[:, o_foxf:o_foxf + H_FOX], w[:, o_gate:o_gate + 3 * H_NSA],
                            jnp.zeros((d, 128 - H_FOX - 3 * H_NSA), w.dtype)], axis=1)
    w_a = jnp.concatenate([w[:, 0:o_foxf], nsaq_cols, w[:, o_kv:o_gate], misc], axis=1).astype(BF16)
    tile = lambda v, n: jnp.tile(v, n)[None, :]
    gains = jnp.concatenate([
        tile(fox_qk_gain[l, 0], 4), tile(fox_qk_gain[l, 1], 4), tile(nsa_qk_gain[l, 0], 4),
        tile(nsa_qk_gain[l, 1], 4), tile(nsa_qk_gain[l, 2], 4), tile(nsa_qk_gain[l, 3], 4),
        jnp.zeros((2, 256), F32)], axis=0)
    fbias = jnp.concatenate([fox_forget_bias[l], jnp.zeros((128 - H_FOX,), F32)])[None, :]
    pool = nsa_cmp_pool[l]
    lanes = lambda p: jnp.repeat(p, HEAD_DIM, axis=1)
    pw = jnp.concatenate([lanes(pool[0]), lanes(pool[1])], axis=1)
    zero = jnp.zeros_like(pw)
    nsa_rows = jnp.concatenate([w_br_nsa[l, 64 * hh:64 * (hh + 1)] for hh in head_order], axis=0)
    return dict(
        attn_norm=attn_norm[l][None, :], w_a=w_a, w_mg=w[:, o_mg:].astype(BF16), bd=cst["bd"], gains=gains,
        fbias=fbias, pwe=jnp.concatenate([pw, zero], axis=0), pwo=jnp.concatenate([zero, pw], axis=0),
        w_br_sb=w_br_sb[l].astype(BF16), w_br_fox=w_br_fox[l].astype(BF16), w_br_nsa=nsa_rows.astype(BF16),
        w_o=w_o[l].astype(BF16), ex=cst["ex"], ffn_norm=ffn_norm[l][None, :], w_up=w_up[l].astype(BF16),
        conv_w=conv_w[l], conv_b=conv_b[l][None, :], w_down=w_down[l].astype(BF16))


def _rope_tables(pos):
    half = ROPE_DIM // 2
    inv = ROPE_THETA ** (-jnp.arange(half, dtype=F32) * (2.0 / ROPE_DIM))
    ang = pos.astype(F32)[:, None] * inv[None, :]
    cos, sin = jnp.cos(ang), jnp.sin(ang)
    t = pos.shape[0]
    one = jnp.ones((t, HEAD_DIM - ROPE_DIM), F32)
    zero8 = jnp.zeros((t, half), F32)
    zero48 = jnp.zeros((t, HEAD_DIM - ROPE_DIM), F32)
    two = lambda a: jnp.concatenate([a, a], axis=1)
    return (two(jnp.concatenate([cos, cos, one], axis=1)),
            two(jnp.concatenate([-sin, zero8, zero48], axis=1)),
            two(jnp.concatenate([zero8, sin, zero48], axis=1)))


def _pad_rows(a, bs, rows):
    t = a.shape[0] // bs
    return jnp.pad(a.reshape(bs, t, a.shape[1]), ((0, 0), (0, rows - t), (0, 0)))


def _prompt_layer(x2, b, t, lw, cst, tabs):
    tm = 512 if t % 512 == 0 else 256
    sbq, sbkv, foxq, foxkv, nsaq, cmpkv, selkv, winkv, misc = _proj(x2, lw, *tabs, tm)
    rt = _logf_sums(misc[:, 0:H_FOX].reshape(-1, PAGE * H_FOX), cst["cmat"])
    o_sb = _attn("sb", sbq, sbkv, b, t, cst["tri256"])
    o_fox = _attn("fox", foxq, foxkv, b, t, rt)
    o_cmp, selm = _cmp_p(nsaq, cmpkv, lw, b, t)
    o_sel = _attn("sel", nsaq, selkv, b, t, selm)
    o_win = _attn("win", nsaq, winkv, b, t)
    x_mid = _merge(x2, o_sb, o_fox, o_cmp, o_sel, o_win, misc, lw, 256)
    d_ff = lw["w_down"].shape[0]
    y, cs = _ffn(x_mid, jnp.zeros((1, 8, d_ff), F32), lw, b, 256)
    n_w = min(WINDOW, t)
    state = (sbkv, foxkv, misc[:, 0:H_FOX], cmpkv, selkv,
             winkv.reshape(b, t, 256)[:, t - n_w:], cs[:, 8 - (CONV_W - 1):])
    return y, state


def _sample_layer(x2, bs, lw, cst, tabs, page_tab, c_sb, c_fox, c_logf, c_cmp, c_sel, st_win, st_conv):
    n = x2.shape[0]
    t_s = n // bs
    sbq, sbkv, foxq, foxkv, nsaq, cmpkv, selkv, winkv, misc = _proj(x2, lw, *tabs, n)
    n_pool = c_sb.shape[0]
    logf_new = misc[:, 0:H_FOX]
    lf_new_pages = _pad_rows(logf_new, bs, PAGE).reshape(bs, PAGE * H_FOX)
    rt_new = _logf_sums(lf_new_pages, cst["cmat"]).reshape(bs, 1, 1536)
    rt_cache = _logf_sums(c_logf.reshape(n_pool, PAGE * H_FOX), cst["cmat"]).reshape(n_pool, 1, 1536)
    o_sb = _dec("sb", sbq, _pad_rows(sbkv, bs, PAGE), c_sb.reshape(n_pool, PAGE, 512), page_tab, cst["tri128"])
    o_fox = _dec("fox", foxq, _pad_rows(foxkv, bs, PAGE), c_fox.reshape(n_pool, PAGE, 512), page_tab,
                 rt_new=rt_new, rt_cache=rt_cache)
    o_cmp, selm = _cmp_s(nsaq, c_cmp.reshape(n_pool, PAGE, 256), page_tab, lw)
    o_sel = _dec("sel", nsaq, _pad_rows(selkv, bs, PAGE), c_sel.reshape(n_pool, PAGE, 256), page_tab, selm)
    n_w = st_win.shape[1]
    win_new = _pad_rows(winkv, bs, PAGE)
    o_win = _win_s(nsaq, st_win.reshape(bs, n_w, 256), win_new, bs)
    x_mid = _merge(x2, o_sb, o_fox, o_cmp, o_sel, o_win, misc, lw, n)
    d_ff = lw["w_down"].shape[0]
    hist = jnp.pad(st_conv, ((0, 0), (8 - (CONV_W - 1), 0), (0, 0)))
    y, cs = _ffn(x_mid, hist, lw, bs, t_s)
    win_state = jnp.concatenate([st_win.reshape(bs, n_w, 256), winkv.reshape(bs, t_s, 256)], axis=1)[:, t_s:]
    state = (sbkv, foxkv, logf_new, cmpkv, selkv, win_state, cs[:, 8 - (CONV_W - 1):])
    return y, state


def kernel(x_prompt, x_sample, cache_sb_kv, cache_fox_kv, cache_fox_logf, cache_nsa_cmp_kv, cache_nsa_sel_kv, state_nsa_win_kv, state_ffn_conv, page_table, attn_norm, w_in, fox_forget_bias, fox_qk_gain, nsa_qk_gain, nsa_cmp_pool, w_br_sb, w_br_fox, w_br_nsa, w_o, ffn_norm, w_up, conv_w, conv_b, w_down):
    b, t, d = x_prompt.shape
    bs, t_s, _ = x_sample.shape
    depth = w_in.shape[0]
    n_pages = page_table.shape[1]
    past = n_pages * PAGE
    assert t % 256 == 0 and t_s == 8 and n_pages % G_PAGES == 0 and past >= WINDOW
    cst = _consts()
    tabs_p = _rope_tables(jnp.arange(t, dtype=jnp.int32))
    tabs_s = tuple(jnp.tile(a, (bs, 1)) for a in _rope_tables(past + jnp.arange(t_s, dtype=jnp.int32)))
    hp = x_prompt.reshape(b * t, d)
    hs = x_sample.reshape(bs * t_s, d)
    st_p, st_s = [], []
    for l in range(depth):
        lw = _layer_weights(l, cst, attn_norm, w_in, fox_forget_bias, fox_qk_gain, nsa_qk_gain, nsa_cmp_pool,
                            w_br_sb, w_br_fox, w_br_nsa, w_o, ffn_norm, w_up, conv_w, conv_b, w_down)
        hp, new_p = _prompt_layer(hp, b, t, lw, cst, tabs_p)
        hs, new_s = _sample_layer(hs, bs, lw, cst, tabs_s, page_table, cache_sb_kv[l], cache_fox_kv[l],
                                  cache_fox_logf[l], cache_nsa_cmp_kv[l], cache_nsa_sel_kv[l],
                                  state_nsa_win_kv[l], state_ffn_conv[l])
        st_p.append(new_p)
        st_s.append(new_s)

    def stk(lst, i, nb, shape):
        return jnp.stack([s[i].reshape((nb, -1) + shape) for s in lst], axis=0)

    outs = [hp.reshape(b, t, d), hs.reshape(bs, t_s, d)]
    shapes = [(2, H_SB, HEAD_DIM), (2, H_FOX, HEAD_DIM), (H_FOX,), (2, HKV_NSA, HEAD_DIM),
              (2, HKV_NSA, HEAD_DIM), (2, HKV_NSA, HEAD_DIM), (d * 3,)]
    for i, shp in enumerate(shapes):
        if i == len(shapes) - 1:
            shp = (st_p[0][i].shape[-1],)
        outs.append(stk(st_p, i, b, shp))
        outs.append(stk(st_s, i, bs, shp))
    return tuple(outs)
```

```python
import functools

import numpy as np
import jax
import jax.numpy as jnp
from jax import lax
from jax.experimental import pallas as pl
from jax.experimental.pallas import tpu as pltpu

F32 = jnp.float32
BF16 = jnp.bfloat16

HEAD_DIM = 64
H_SB = 4
H_FOX = 4
H_NSA = 8
HKV_NSA = 2
GQA_R = H_NSA // HKV_NSA
ROPE_DIM = HEAD_DIM // 4
ROPE_THETA = 500000.0
CMP_BLOCK = 32
SEL_BLOCK = 64
SEL_TOPK = 16
WINDOW = 512
CONV_W = 3
EPS = 1e-6
SCALE = HEAD_DIM ** -0.5
NEG_INF = -1e30
TINY = 1e-30
FORCE_SCORE = 1e6
PAGE = 128
LANES = 128
VMEM_LIMIT = 56 * 1024 * 1024
SB_DEAD = 110.0

C_SBQ, C_SBKV, C_FOXQ, C_FOXK, C_FOXV = 0, 256, 768, 1024, 1280
C_NSAQ, C_CMP, C_SEL, C_WIN, C_MISC, C_END = 1536, 2048, 2304, 2560, 2816, 2944
N_MG = 3 * 1024


def _dot(a, b):
    return jnp.dot(a, b, preferred_element_type=F32)


def _dot_nt(a, b):
    return lax.dot_general(a, b, (((1,), (1,)), ((), ())), preferred_element_type=F32)


def _split2(x):
    hi = x.astype(BF16)
    lo = (x - hi.astype(F32)).astype(BF16)
    return hi, lo


def _split3(x):
    hi = x.astype(BF16)
    r = x - hi.astype(F32)
    mid = r.astype(BF16)
    lo = (r - mid.astype(F32)).astype(BF16)
    return hi, mid, lo


def _softplus(z):
    return jnp.maximum(z, 0.0) + jnp.log(1.0 + jnp.exp(-jnp.abs(z)))


def _iota(shape, dim):
    return lax.broadcasted_iota(jnp.int32, shape, dim)


def _cparams(sem):
    return pltpu.CompilerParams(dimension_semantics=sem, vmem_limit_bytes=VMEM_LIMIT)


def _proj_kernel(feature_major, x_ref, g_ref, w_ref, bd_ref, gains_ref, cos_ref, sa_ref, sb_ref, fb_ref, *outs):
    x = x_ref[...]
    h = x * lax.rsqrt(jnp.mean(x * x, axis=-1, keepdims=True) + EPS) * g_ref[...]
    hb = h.astype(BF16)

    def seg(lo, n):
        return _dot(hb, w_ref[:, lo:lo + n])

    def hrms(z, gain):
        w = z.shape[1]
        hi, lo = _split2(z * z)
        bd = bd_ref[0:w, 0:w]
        ss = _dot(hi, bd) + _dot(lo, bd)
        return z * lax.rsqrt(ss * (1.0 / HEAD_DIM) + EPS) * gain

    def rope(z):
        return (z * cos_ref[...] + pltpu.roll(z, LANES - ROPE_DIM // 2, 1) * sa_ref[...]
                + pltpu.roll(z, ROPE_DIM // 2, 1) * sb_ref[...])

    sb_q = seg(C_SBQ, 256) * SCALE
    sb_kv = seg(C_SBKV, 512)
    fox_q = hrms(seg(C_FOXQ, 256), gains_ref[0:1, :]) * SCALE
    fox_k = hrms(seg(C_FOXK, 256), gains_ref[1:2, :])
    fox_v = seg(C_FOXV, 256)
    nsa_q = []
    for c in range(2):
        zq = hrms(seg(C_NSAQ + 256 * c, 256), gains_ref[2:3, :])
        nsa_q += [rope(zq[:, 128 * s:128 * (s + 1)]) * SCALE for s in range(2)]
    nsa_k, nsa_v = [], []
    for i, col in enumerate((C_CMP, C_SEL, C_WIN)):
        nsa_k.append(rope(hrms(seg(col, 128), gains_ref[3 + i:4 + i, 0:128])))
        nsa_v.append(seg(col + 128, 128))
    zm = seg(C_MISC, 128)
    lane = _iota((1, LANES), 1)
    misc = jnp.where(lane < H_FOX, -_softplus(-(zm + fb_ref[...])), jax.nn.sigmoid(zm))

    if not feature_major:
        sbq_ref, sbkv_ref, foxq_ref, foxkv_ref, nsaq_ref, cmp_ref, sel_ref, win_ref, misc_ref = outs
        sbq_ref[...] = sb_q.astype(BF16)
        sbkv_ref[...] = sb_kv
        foxq_ref[...] = fox_q.astype(BF16)
        foxkv_ref[:, 0:256] = fox_k
        foxkv_ref[:, 256:512] = fox_v
        for s in range(4):
            nsaq_ref[:, 128 * s:128 * (s + 1)] = nsa_q[s].astype(BF16)
        for i, ref in enumerate((cmp_ref, sel_ref, win_ref)):
            ref[:, 0:128] = nsa_k[i]
            ref[:, 128:256] = nsa_v[i]
        misc_ref[...] = misc
        return

    (sbk_ref, foxk_ref, nsak_ref, nsaq_ref, cmpkv_ref, misc_ref,
     sbqt_ref, foxqt_ref, nsaqt_ref, sbkvt_ref, foxkvt_ref, cmpt_ref, selt_ref, wint_ref, lft_ref) = outs
    sbk_ref[...] = sb_kv[:, 0:256].astype(BF16)
    foxk_ref[...] = fox_k.astype(BF16)
    for i in range(3):
        nsak_ref[:, 128 * i:128 * (i + 1)] = nsa_k[i].astype(BF16)
    for s in range(4):
        nsaq_ref[:, 128 * s:128 * (s + 1)] = nsa_q[s].astype(BF16)
        nsaqt_ref[0, 128 * s:128 * (s + 1), :] = nsa_q[s].T.astype(BF16)
    cmpkv_ref[:, 0:128] = nsa_k[0]
    cmpkv_ref[:, 128:256] = nsa_v[0]
    misc_ref[...] = misc
    sbqt_ref[0] = sb_q.T.astype(BF16)
    foxqt_ref[0] = fox_q.T.astype(BF16)
    sbkvt_ref[0] = sb_kv.T
    foxkvt_ref[0, 0:256, :] = fox_k.T
    foxkvt_ref[0, 256:512, :] = fox_v.T
    for i, ref in enumerate((cmpt_ref, selt_ref, wint_ref)):
        ref[0, 0:128, :] = nsa_k[i].T
        ref[0, 128:256, :] = nsa_v[i].T
    lft_ref[0] = misc.T[0:8, :]


def _proj(x2, lw, cos_t, sa_t, sb_t, tm, nb=None):
    n, d = x2.shape
    nt = cos_t.shape[0] // tm
    row = lambda i: (i, 0)
    const = lambda i: (0, 0)
    tab = pl.BlockSpec((tm, LANES), lambda i: (i % nt, 0))
    if nb is None:
        outs = [(256, BF16), (512, F32), (256, BF16), (512, F32), (512, BF16),
                (256, F32), (256, F32), (256, F32), (128, F32)]
        out_specs = [pl.BlockSpec((tm, w), row) for w, _ in outs]
        out_shape = [jax.ShapeDtypeStruct((n, w), dt) for w, dt in outs]
    else:
        t = n // nb
        nat = [(256, BF16), (256, BF16), (384, BF16), (512, BF16), (256, F32), (128, F32)]
        fm = [(256, BF16), (256, BF16), (512, BF16), (512, F32), (512, F32), (256, F32), (256, F32), (256, F32),
              (8, F32)]
        fmap = lambda i: (i // nt, 0, i % nt)
        out_specs = ([pl.BlockSpec((tm, w), row) for w, _ in nat]
                     + [pl.BlockSpec((1, w, tm), fmap) for w, _ in fm])
        out_shape = ([jax.ShapeDtypeStruct((n, w), dt) for w, dt in nat]
                     + [jax.ShapeDtypeStruct((nb, w, t), dt) for w, dt in fm])
    return pl.pallas_call(
        functools.partial(_proj_kernel, nb is not None),
        grid=(n // tm,),
        in_specs=[pl.BlockSpec((tm, d), row), pl.BlockSpec((1, d), const),
                  pl.BlockSpec((d, C_END), const), pl.BlockSpec((256, 256), const),
                  pl.BlockSpec((8, 256), const), tab, tab, tab, pl.BlockSpec((1, LANES), const)],
        out_specs=out_specs,
        out_shape=out_shape,
        compiler_params=_cparams(("parallel",)),
        name="proj",
    )(x2, lw["attn_norm"], lw["w_a"], lw["bd"], lw["gains"], cos_t, sa_t, sb_t, lw["fbias"])


def _logf_kernel(x_ref, c_ref, o_ref):
    parts = _split3(x_ref[...])
    c = c_ref[...]
    o_ref[...] = _dot(parts[0], c) + _dot(parts[1], c) + _dot(parts[2], c)


def _logf_sums(lf_pages, cmat):
    p = lf_pages.shape[0]
    tp = 256 if p % 256 == 0 else p
    return pl.pallas_call(
        _logf_kernel,
        grid=(p // tp,),
        in_specs=[pl.BlockSpec((tp, 512), lambda i: (i, 0)), pl.BlockSpec((512, 1536), lambda i: (0, 0))],
        out_specs=pl.BlockSpec((tp, 1536), lambda i: (i, 0)),
        out_shape=jax.ShapeDtypeStruct((p, 1536), F32),
        compiler_params=_cparams(("parallel",)),
        name="logf",
    )(lf_pages, cmat)


def _attn_kernel(mode, nsa, tq, tk, n_s, qi_ref, kb_ref, *refs):
    refs = list(refs)
    qt_ref, k_ref, vt_ref = refs[0:3]
    pos = 3
    tri_ref = lf_ref = sm_ref = None
    if mode in ("sb", "fox"):
        tri_ref = refs[pos]; pos += 1
    if mode == "fox":
        lf_ref = refs[pos]; pos += 1
    if mode == "sel":
        sm_ref = refs[pos]; pos += 1
    o_ref, acc_ref, m_ref = refs[pos:pos + 3]
    cf_ref = refs[pos + 3] if mode == "fox" else None
    dead_ref = refs[pos + 3] if mode == "sb" else None
    nh = H_NSA if nsa else H_SB

    p_id = pl.program_id(1)
    qi = qi_ref[p_id]
    kb = kb_ref[p_id]
    first = kb == qi
    w_blocks = WINDOW // tk
    last = (kb == jnp.maximum(qi - w_blocks, 0)) if mode == "win" else (kb == 0)

    @pl.when(first)
    def _init():
        acc_ref[...] = jnp.zeros_like(acc_ref)
        m_ref[...] = jnp.zeros_like(m_ref) if mode == "sb" else jnp.full_like(m_ref, NEG_INF)
        if mode == "fox":
            cf_ref[...] = jnp.zeros_like(cf_ref)
        if mode == "sb":
            dead_ref[0] = 0

    row_lo = _iota((LANES, 1), 0) < HEAD_DIM

    def body(edge):
        mask = None
        if edge:
            d = (qi * tq + _iota((tk, tq), 1)) - (kb * tk + _iota((tk, tq), 0))
            if mode == "sb":
                mask = d > 0
            elif mode == "win":
                mask = (d >= 0) & (d < WINDOW)
            else:
                mask = d >= 0
        masks = [mask, mask]
        if mode == "sel":
            sm = sm_ref[0]
            spb = tk // SEL_BLOCK
            for g in range(2):
                e = (_iota((tk, 2 * n_s), 1) == g * n_s + kb * spb + _iota((tk, 2 * n_s), 0) // SEL_BLOCK)
                picked = _dot(e.astype(BF16), sm) > 0.5
                masks[g] = picked if mask is None else (mask & picked)
        rfull = None
        if mode == "fox":
            lf = lf_ref[...]
            tri = tri_ref[...]
            p3 = _split3(lf)
            rb = _dot(tri, p3[0]) + _dot(tri, p3[1]) + _dot(tri, p3[2])
            cf = cf_ref[0:1, :]
            rfull = rb + cf
            cf_ref[...] = jnp.broadcast_to(cf + rb[0:1, :] + lf[0:1, :], cf_ref.shape)

        hms, vts, ss = [], [], []
        for n in range(nh):
            slab, sub = divmod(n, 2)
            ko = 0 if nsa else 128 * slab
            hm = row_lo if sub == 0 else jnp.logical_not(row_lo)
            qtm = jnp.where(hm, qt_ref[128 * slab:128 * (slab + 1), :], jnp.zeros((), BF16))
            ss.append(_dot(k_ref[:, ko:ko + 128], qtm))
            vo = 0 if nsa else 128 * slab
            vts.append(vt_ref[0, 0, vo:vo + 128, :].astype(BF16))
            hms.append(hm)
        if mode == "sb":
            tri = tri_ref[...]
            sps, spms, hls = [], [], []
            for n in range(nh):
                sp = _softplus(ss[n])
                spm = sp if mask is None else jnp.where(mask, sp, 0.0)
                sps.append(sp)
                spms.append(spm)
                hls.append(_split2(spm))
            als = [_dot(tri, hi) + _dot(tri, lo) for hi, lo in hls]
            avs = []
            for n in range(nh):
                carry = m_ref[n]
                a = jnp.exp((ss[n] - sps[n]) - (als[n] + carry[0:1, :]))
                if mask is not None:
                    a = jnp.where(mask, a, 0.0)
                avs.append(a.astype(BF16))
                m_ref[n] = carry + (als[n][0:1, :] + spms[n][0:1, :])
            for n in range(nh):
                acc_ref[n] += _dot(vts[n], avs[n])
            dead_ref[0] = (jnp.min(m_ref[...]) > SB_DEAD).astype(jnp.int32)
        else:
            ps, alphas = [], []
            for n in range(nh):
                s = ss[n]
                if mode == "fox":
                    s = s + rfull[:, n:n + 1]
                if masks[n % 2] is not None:
                    s = jnp.where(masks[n % 2], s, NEG_INF)
                m_old = m_ref[n]
                m_new = jnp.maximum(m_old, jnp.max(s, axis=0, keepdims=True))
                ps.append(jnp.exp(s - m_new[0:1, :]).astype(BF16))
                alphas.append(jnp.exp(m_old - m_new))
                m_ref[n] = m_new
            for n in range(nh):
                vaug = jnp.where(hms[n], vts[n], jnp.ones((), BF16))
                acc_ref[n] = acc_ref[n] * alphas[n][0:1, :] + _dot(vaug, ps[n])

    if mode == "win":
        edge = first | (kb == qi - w_blocks)
    else:
        edge = first
    live = (dead_ref[0] == 0) if mode == "sb" else True
    @pl.when(edge & live)
    def _edge():
        body(True)

    @pl.when(jnp.logical_not(edge) & live)
    def _inner():
        body(False)

    @pl.when(last)
    def _fin():
        for slab in range(nh // 2):
            a0, a1 = acc_ref[2 * slab], acc_ref[2 * slab + 1]
            if mode == "sb":
                o0, o1 = a0[0:HEAD_DIM, :], a1[HEAD_DIM:LANES, :]
            else:
                o0 = a0[0:HEAD_DIM, :] / a0[HEAD_DIM:HEAD_DIM + 1, :]
                o1 = a1[HEAD_DIM:LANES, :] / a1[0:1, :]
            o_ref[:, 128 * slab:128 * (slab + 1)] = jnp.concatenate([o0, o1], axis=0).T


def _pairs(nq, mode, tk):
    qi, kb = [], []
    for i in range(nq):
        lo = max(i - WINDOW // tk, 0) if mode == "win" else 0
        for k in range(i, lo - 1, -1):
            qi.append(i)
            kb.append(k)
    return jnp.asarray(qi, jnp.int32), jnp.asarray(kb, jnp.int32)


def _attn(mode, qt, k, kvt, b, t, kcol=0, extra=None, tri=None, tq=256, tk=256):
    nsa = mode in ("sel", "win")
    wq = qt.shape[1]
    wv = kvt.shape[1] // 2
    wk = 128 if nsa else 256
    nq = t // tq
    n_s = t // SEL_BLOCK
    qi_tab, kb_tab = _pairs(nq, mode, tk)
    in_specs = [pl.BlockSpec((1, wq, tq), lambda bi, p, qi, kb: (bi, 0, qi[p])),
                pl.BlockSpec((tk, wk), lambda bi, p, qi, kb: (bi * (t // tk) + kb[p], kcol)),
                pl.BlockSpec((1, 1, wv, tk), lambda bi, p, qi, kb: (bi, 1, 0, kb[p]))]
    args = [qt, k, kvt.reshape(b, 2, wv, t)]
    if mode in ("sb", "fox"):
        in_specs.append(pl.BlockSpec((tk, tk), lambda bi, p, qi, kb: (0, 0)))
        args.append(tri)
    if mode == "fox":
        in_specs.append(pl.BlockSpec((tk, 128), lambda bi, p, qi, kb: (bi * (t // tk) + kb[p], 0)))
        args.append(extra)
    if mode == "sel":
        in_specs.append(pl.BlockSpec((1, 2 * n_s, tq), lambda bi, p, qi, kb: (bi, 0, qi[p])))
        args.append(extra)
    nh = H_NSA if nsa else H_SB
    scratch = [pltpu.VMEM((nh, 128, tq), F32), pltpu.VMEM((nh, 8, tq), F32)]
    if mode == "fox":
        scratch.append(pltpu.VMEM((8, 128), F32))
    if mode == "sb":
        scratch.append(pltpu.SMEM((1,), jnp.int32))

    def kernel(qi_ref, kb_ref, qt_ref, *rest):
        _attn_kernel(mode, nsa, tq, tk, n_s, qi_ref, kb_ref, qt_ref.at[0], *rest)

    return pl.pallas_call(
        kernel,
        grid_spec=pltpu.PrefetchScalarGridSpec(
            num_scalar_prefetch=2, grid=(b, int(qi_tab.shape[0])),
            in_specs=in_specs, out_specs=pl.BlockSpec((tq, wq), lambda bi, p, qi, kb: (bi * nq + qi[p], 0)),
            scratch_shapes=scratch),
        out_shape=jax.ShapeDtypeStruct((b * t, wq), F32),
        compiler_params=_cparams(("parallel", "arbitrary")),
        name="attn_" + mode,
    )(qi_tab, kb_tab, *args)


def _pool_blocks(x, pwe_ref, pwo_ref):
    n = x.shape[0] // SEL_BLOCK
    x3 = x.reshape(n, SEL_BLOCK, 256)
    return (jnp.sum(x3 * pwe_ref[...][None], axis=1), jnp.sum(x3 * pwo_ref[...][None], axis=1))


def _rank_select(sc, n_blk, k_sel):
    j = _iota((n_blk, 1), 0)
    rank = jnp.zeros(sc.shape, F32)
    for i in range(n_blk):
        ri = sc[i:i + 1, :]
        beats = (ri > sc) | ((ri == sc) & (j > i))
        rank = rank + beats.astype(F32)
    return (rank < k_sel).astype(F32)


def _cmp_p_kernel(tq, t, q_ref, ckv_ref, pwe_ref, pwo_ref, o_ref, sm_ref, kc_ref):
    n_s = t // SEL_BLOCK
    n_c = 2 * n_s
    qi = pl.program_id(1)

    @pl.when(qi == 0)
    def _pool():
        ev, od = _pool_blocks(ckv_ref[...], pwe_ref, pwo_ref)
        kc_ref[0:n_s, :] = ev
        kc_ref[n_s:n_c, :] = od

    kcb = kc_ref[:, 0:128].astype(BF16)
    vcb = kc_ref[:, 128:256].astype(BF16)

    def c_end(i):
        blk = jnp.where(i < n_s, 2 * i, 2 * (i - n_s) + 1)
        return (blk + 1) * CMP_BLOCK - 1

    cmask = c_end(_iota((1, n_c), 1)) <= qi * tq + _iota((tq, 1), 0)
    t_row = qi * tq + _iota((1, tq), 1)
    cmask_t = c_end(_iota((n_c, 1), 0)) <= t_row
    lane_lo = _iota((1, LANES), 1) < HEAD_DIM
    imp = [jnp.zeros((n_s, tq), F32), jnp.zeros((n_s, tq), F32)]
    outs = {}
    for n in range(H_NSA):
        r, g = divmod(n, 2)
        hm = lane_lo if g == 0 else jnp.logical_not(lane_lo)
        qh = jnp.where(hm, q_ref[:, 128 * r:128 * (r + 1)], jnp.zeros((), BF16))
        s = jnp.where(cmask, _dot_nt(qh, kcb), NEG_INF)
        e = jnp.where(cmask, jnp.exp(s - jnp.max(s, axis=-1, keepdims=True)), 0.0)
        p = e / jnp.maximum(jnp.sum(e, axis=-1, keepdims=True), TINY)
        outs[(r, g)] = _dot(p.astype(BF16), vcb)
        st = jnp.where(cmask_t, _dot_nt(kcb, qh), NEG_INF)
        et = jnp.where(cmask_t, jnp.exp(st - jnp.max(st, axis=0, keepdims=True)), 0.0)
        pt = et / jnp.maximum(jnp.sum(et, axis=0, keepdims=True), TINY)
        imp[g] = imp[g] + (pt[0:n_s, :] + pt[n_s:n_c, :])
    for r in range(GQA_R):
        o_ref[:, 128 * r:128 * (r + 1)] = jnp.where(lane_lo, outs[(r, 0)], outs[(r, 1)])

    j = _iota((n_s, 1), 0)
    cur = t_row // SEL_BLOCK
    forced = (j == 0) | (j == cur) | (j == cur - 1)
    valid = j * SEL_BLOCK <= t_row
    sel = []
    for g in range(2):
        sc = jnp.where(forced, FORCE_SCORE, jnp.where(valid, imp[g], -1.0))
        sel.append(_rank_select(sc, n_s, min(SEL_TOPK, n_s)))
    sm_ref[0] = jnp.concatenate(sel, axis=0).astype(BF16)


def _cmp_p(nsaq, ckv, lw, b, t, tq=256):
    n = nsaq.shape[0]
    nq = t // tq
    n_s = t // SEL_BLOCK
    rowmap = lambda bi, qi: (bi * nq + qi, 0)
    const = lambda bi, qi: (0, 0)
    return pl.pallas_call(
        functools.partial(_cmp_p_kernel, tq, t),
        grid=(b, nq),
        in_specs=[pl.BlockSpec((tq, 512), rowmap), pl.BlockSpec((t, 256), lambda bi, qi: (bi, 0)),
                  pl.BlockSpec((SEL_BLOCK, 256), const), pl.BlockSpec((SEL_BLOCK, 256), const)],
        out_specs=[pl.BlockSpec((tq, 512), rowmap), pl.BlockSpec((1, 2 * n_s, tq), lambda bi, qi: (bi, 0, qi))],
        out_shape=[jax.ShapeDtypeStruct((n, 512), F32), jax.ShapeDtypeStruct((b, 2 * n_s, t), BF16)],
        scratch_shapes=[pltpu.VMEM((2 * n_s, 256), F32)],
        compiler_params=_cparams(("parallel", "arbitrary")),
        name="cmp_p",
    )(nsaq, ckv, lw["pwe"], lw["pwo"])


G_PAGES = 8


def _dec_kernel(mode, n_pages, n_sp, pt_ref, *refs):
    refs = list(refs)
    nsa = mode == "sel"
    g_n = G_PAGES
    q_ref, new_ref = refs[0], refs[1]
    pages = refs[2:2 + g_n]
    pos = 2 + g_n
    tri_ref = sm_ref = rtn_ref = None
    rts = None
    if mode == "sb":
        tri_ref = refs[pos]; pos += 1
    if mode == "fox":
        rtn_ref = refs[pos]; pos += 1
        rts = refs[pos:pos + g_n]; pos += g_n
    if mode == "sel":
        sm_ref = refs[pos]; pos += 1
    o_ref, qbd_ref, acc_ref, m_ref, l_ref = refs[pos:pos + 5]
    cf_ref = refs[pos + 5] if mode == "fox" else None
    dead_ref = refs[pos + 5] if mode == "sb" else None

    t_s = q_ref.shape[0]
    nh = H_NSA if nsa else H_SB
    rows = nh * t_s
    kw = 128 if nsa else 256
    step = pl.program_id(1)
    n_steps = pl.num_programs(1)
    lane_lo = _iota((1, LANES), 1) < HEAD_DIM
    t_of_row = _iota((rows, 1), 0) % t_s

    def head_lanes(n, width):
        ln = _iota((1, width), 1)
        return (ln >= HEAD_DIM * n) & (ln < HEAD_DIM * (n + 1))

    def block(kt32, vt32, mask, bias):
        ktb = kt32.astype(BF16)
        vtb = vt32.astype(BF16)
        s = _dot(qbd_ref[...], ktb)
        if mode == "sb":
            sp = _softplus(s)
            spm = sp if mask is None else jnp.where(mask, sp, 0.0)
            tri = tri_ref[...]
            later = m_ref[...]
            als = []
            for c in reversed(range(s.shape[1] // PAGE)):
                spc = spm[:, PAGE * c:PAGE * (c + 1)]
                hi, lo = _split2(spc)
                alc = _dot(hi, tri) + _dot(lo, tri)
                als.append(alc + later)
                later = later + alc[:, 0:1] + spc[:, 0:1]
            al = als[0] if len(als) == 1 else jnp.concatenate(als[::-1], axis=1)
            a = jnp.exp((s - sp) - al)
            if mask is not None:
                a = jnp.where(mask, a, 0.0)
            acc_ref[...] += _dot_nt(a.astype(BF16), vtb)
            m_ref[...] = later
        else:
            if bias is not None:
                s = s + bias
            if mask is not None:
                s = jnp.where(mask, s, NEG_INF)
            m_old = m_ref[...]
            m_new = jnp.maximum(m_old, jnp.max(s, axis=-1, keepdims=True))
            p = jnp.exp(s - m_new)
            alpha = jnp.exp(m_old - m_new)
            l_ref[...] = alpha * l_ref[...] + jnp.sum(p, axis=-1, keepdims=True)
            acc_ref[...] = alpha * acc_ref[...] + _dot_nt(p.astype(BF16), vtb)
            m_ref[...] = m_new

    def rows_of(vecs):
        return jnp.concatenate([jnp.broadcast_to(v, (t_s, 128)) for v in vecs], axis=0)

    @pl.when(step == 0)
    def _first():
        q = q_ref[...]
        if nsa:
            parts = []
            for g in range(HKV_NSA):
                hm = lane_lo if g == 0 else jnp.logical_not(lane_lo)
                for r in range(GQA_R):
                    parts.append(jnp.where(hm, q[:, 128 * r:128 * (r + 1)], jnp.zeros((), BF16)))
        else:
            parts = [jnp.where(head_lanes(n, 256), q, jnp.zeros((), BF16)) for n in range(nh)]
        qbd_ref[...] = jnp.concatenate(parts, axis=0)
        acc_ref[...] = jnp.zeros_like(acc_ref)
        l_ref[...] = jnp.zeros_like(l_ref)
        m_ref[...] = jnp.zeros_like(m_ref) if mode == "sb" else jnp.full_like(m_ref, NEG_INF)
        key = _iota((rows, PAGE), 1)
        mask = (key < t_of_row) if mode == "sb" else (key <= t_of_row)
        bias = None
        if mode == "fox":
            cf_ref[...] = jnp.zeros_like(cf_ref)
            bias = -rows_of([rtn_ref[0, :, 512 + 128 * n:512 + 128 * (n + 1)] for n in range(nh)])
        if mode == "sb":
            dead_ref[0] = 0
        block(new_ref[0, 0:kw, :], new_ref[0, kw:2 * kw, :], mask, bias)

    live = (dead_ref[0] == 0) if mode == "sb" else True

    @pl.when((step > 0) & live)
    def _pages():
        bias = None
        mask = None
        if mode == "fox":
            later = cf_ref[...]
            pieces = []
            for g in reversed(range(g_n)):
                rt = rts[g]
                pieces.append(rows_of([rt[0, :, 128 * n:128 * (n + 1)] for n in range(nh)]) + later)
                later = later + rows_of([rt[0, :, 1024 + 128 * n:1024 + 128 * (n + 1)] for n in range(nh)])
            bias = jnp.concatenate(pieces[::-1], axis=1)
            cf_ref[...] = later
        if mode == "sel":
            nk = g_n * PAGE
            first_pos = (n_pages - g_n * step) * PAGE
            blk = (first_pos + _iota((2 * n_sp, nk), 1)) // SEL_BLOCK
            e = (_iota((2 * n_sp, nk), 0) % n_sp) == blk
            grp = _iota((rows, 2 * n_sp), 0) // (GQA_R * t_s) == _iota((rows, 2 * n_sp), 1) // n_sp
            smr = jnp.concatenate([sm_ref[...]] * nh, axis=0)
            smr = jnp.where(grp, smr, jnp.zeros((), BF16))
            mask = _dot(smr, e.astype(BF16)) > 0.5
        kt = jnp.concatenate([p[0, 0:kw, :] for p in pages], axis=1)
        vt = jnp.concatenate([p[0, kw:2 * kw, :] for p in pages], axis=1)
        block(kt, vt, mask, bias)
        if mode == "sb":
            dead_ref[0] = (jnp.min(m_ref[...]) > SB_DEAD).astype(jnp.int32)

    @pl.when(step == n_steps - 1)
    def _fin():
        o = acc_ref[...] if mode == "sb" else acc_ref[...] / l_ref[...]
        if nsa:
            for r in range(GQA_R):
                o0 = o[t_s * r:t_s * (r + 1), :]
                o1 = o[t_s * (GQA_R + r):t_s * (GQA_R + r + 1), :]
                o_ref[:, 128 * r:128 * (r + 1)] = jnp.where(lane_lo, o0, o1)
        else:
            out = jnp.zeros((t_s, 256), F32)
            for n in range(nh):
                out = out + jnp.where(head_lanes(n, 256), o[t_s * n:t_s * (n + 1), :], 0.0)
            o_ref[...] = out


def _dec(mode, q, newkv_t, cache_t, layer, page_tab, extra=None, rt_new=None, rt_cache=None):
    nsa = mode == "sel"
    bs, n_pages = page_tab.shape
    t_s = q.shape[0] // bs
    wq = q.shape[1]
    wkv = cache_t.shape[1]
    g_n = G_PAGES
    n_steps = 1 + n_pages // g_n
    n_sp = n_pages * PAGE // SEL_BLOCK
    nh = H_NSA if nsa else H_SB
    rows = nh * t_s

    def page_map(g, base):
        def f(b, s, pt):
            return (base + pt[b * n_pages + n_pages - g_n * jnp.maximum(s, 1) + g], 0, 0)
        return f

    base = layer * (cache_t.shape[0] // 2)
    in_specs = [pl.BlockSpec((t_s, wq), lambda b, s, pt: (b, 0)),
                pl.BlockSpec((1, wkv, PAGE), lambda b, s, pt: (b, 0, 0))]
    in_specs += [pl.BlockSpec((1, wkv, PAGE), page_map(g, base)) for g in range(g_n)]
    args = [q, newkv_t] + [cache_t] * g_n
    if mode == "sb":
        in_specs.append(pl.BlockSpec((PAGE, PAGE), lambda b, s, pt: (0, 0)))
        args.append(extra)
    if mode == "fox":
        in_specs.append(pl.BlockSpec((1, 1, 1536), lambda b, s, pt: (b, 0, 0)))
        args.append(rt_new)
        in_specs += [pl.BlockSpec((1, 1, 1536), page_map(g, 0)) for g in range(g_n)]
        args += [rt_cache] * g_n
    if mode == "sel":
        in_specs.append(pl.BlockSpec((t_s, 2 * n_sp), lambda b, s, pt: (b, 0)))
        args.append(extra)
    scratch = [pltpu.VMEM((rows, 128 if nsa else 256), BF16), pltpu.VMEM((rows, 128 if nsa else 256), F32),
               pltpu.VMEM((rows, 1), F32), pltpu.VMEM((rows, 1), F32)]
    if mode == "fox":
        scratch.append(pltpu.VMEM((rows, 128), F32))
    if mode == "sb":
        scratch.append(pltpu.SMEM((1,), jnp.int32))
    return pl.pallas_call(
        functools.partial(_dec_kernel, mode, n_pages, n_sp),
        grid_spec=pltpu.PrefetchScalarGridSpec(
            num_scalar_prefetch=1, grid=(bs, n_steps), in_specs=in_specs,
            out_specs=pl.BlockSpec((t_s, wq), lambda b, s, pt: (b, 0)), scratch_shapes=scratch),
        out_shape=jax.ShapeDtypeStruct((bs * t_s, wq), F32),
        compiler_params=_cparams(("parallel", "arbitrary")),
        name="dec_" + mode,
    )(page_tab.reshape(-1), *args)


def _cmp_s_kernel(n_pages, pt_ref, q_ref, *refs):
    g_n = G_PAGES
    pages = refs[0:g_n]
    wph_ref, wpl_ref, o_ref, sm_ref, kc_ref = refs[g_n:g_n + 5]
    t_s = q_ref.shape[0]
    n_sp = n_pages * PAGE // SEL_BLOCK
    n_c = 2 * n_sp
    per = g_n * PAGE // SEL_BLOCK
    step = pl.program_id(1)
    x = jnp.concatenate([p[0] for p in pages], axis=1)
    xh, xl = _split2(x)
    pooled = []
    for r in range(4):
        sl = slice(HEAD_DIM * r, HEAD_DIM * (r + 1))
        pooled.append(_dot(xh[sl], wph_ref[r]) + _dot(xl[sl], wph_ref[r]) + _dot(xh[sl], wpl_ref[r]))
    pt = jnp.concatenate(pooled, axis=0).T
    off = pl.multiple_of(step * per, per)
    kc_ref[pl.ds(off, per), :] = pt[0:per, :]
    kc_ref[pl.ds(n_sp + off, per), :] = pt[per:2 * per, :]

    @pl.when(step == pl.num_programs(1) - 1)
    def _fin():
        kcb = kc_ref[:, 0:128].astype(BF16)
        vcb = kc_ref[:, 128:256].astype(BF16)
        lane_lo = _iota((1, LANES), 1) < HEAD_DIM
        q = q_ref[...]
        imps = []
        outs = []
        for g in range(HKV_NSA):
            hm = lane_lo if g == 0 else jnp.logical_not(lane_lo)
            qg = jnp.concatenate([jnp.where(hm, q[:, 128 * r:128 * (r + 1)], jnp.zeros((), BF16))
                                  for r in range(GQA_R)], axis=0)
            s = _dot_nt(qg, kcb)
            e = jnp.exp(s - jnp.max(s, axis=-1, keepdims=True))
            p = e / jnp.maximum(jnp.sum(e, axis=-1, keepdims=True), TINY)
            outs.append(_dot(p.astype(BF16), vcb))
            imp = jnp.zeros((t_s, n_sp), F32)
            for r in range(GQA_R):
                pr = p[t_s * r:t_s * (r + 1), :]
                imp = imp + (pr[:, 0:n_sp] + pr[:, n_sp:n_c])
            imps.append(imp)
        for r in range(GQA_R):
            o_ref[:, 128 * r:128 * (r + 1)] = jnp.where(
                lane_lo, outs[0][t_s * r:t_s * (r + 1), :], outs[1][t_s * r:t_s * (r + 1), :])
        imp = jnp.concatenate(imps, axis=0)
        jb = _iota((2 * t_s, n_sp), 1)
        sc = jnp.where((jb == 0) | (jb == n_sp - 1), FORCE_SCORE, imp)
        chosen = jnp.zeros(sc.shape, F32)
        for _ in range(min(SEL_TOPK - 1, n_sp)):
            idx = jnp.argmax(sc, axis=1).astype(jnp.int32)[:, None]
            hit = jb == idx
            chosen = jnp.where(hit, 1.0, chosen)
            sc = jnp.where(hit, -jnp.inf, sc)
        sm_ref[...] = jnp.concatenate([chosen[0:t_s], chosen[t_s:2 * t_s]], axis=1).astype(BF16)


def _cmp_s(nsaq, cache_t, layer, page_tab, lw):
    bs, n_pages = page_tab.shape
    t_s = nsaq.shape[0] // bs
    g_n = G_PAGES
    n_sp = n_pages * PAGE // SEL_BLOCK
    base = layer * (cache_t.shape[0] // 2)

    def page_map(g):
        return lambda b, s, pt: (base + pt[b * n_pages + g_n * s + g], 0, 0)

    const3 = lambda b, s, pt: (0, 0, 0)
    return pl.pallas_call(
        functools.partial(_cmp_s_kernel, n_pages),
        grid_spec=pltpu.PrefetchScalarGridSpec(
            num_scalar_prefetch=1, grid=(bs, n_pages // g_n),
            in_specs=[pl.BlockSpec((t_s, 512), lambda b, s, pt: (b, 0))]
                     + [pl.BlockSpec((1, 256, PAGE), page_map(g)) for g in range(g_n)]
                     + [pl.BlockSpec((4, g_n * PAGE, 128), const3), pl.BlockSpec((4, g_n * PAGE, 128), const3)],
            out_specs=[pl.BlockSpec((t_s, 512), lambda b, s, pt: (b, 0)),
                       pl.BlockSpec((t_s, 2 * n_sp), lambda b, s, pt: (b, 0))],
            scratch_shapes=[pltpu.VMEM((2 * n_sp, 256), F32)]),
        out_shape=[jax.ShapeDtypeStruct((bs * t_s, 512), F32), jax.ShapeDtypeStruct((bs * t_s, 2 * n_sp), BF16)],
        compiler_params=_cparams(("parallel", "arbitrary")),
        name="cmp_s",
    )(page_tab.reshape(-1), nsaq, *([cache_t] * g_n), lw["wph"], lw["wpl"])


def _win_s_kernel(q_ref, st_ref, new_ref, o_ref):
    t_s = q_ref.shape[0]
    n_w = st_ref.shape[2]
    rows = H_NSA * t_s
    lane_lo = _iota((1, LANES), 1) < HEAD_DIM
    q = q_ref[...]
    parts = []
    for g in range(HKV_NSA):
        hm = lane_lo if g == 0 else jnp.logical_not(lane_lo)
        for r in range(GQA_R):
            parts.append(jnp.where(hm, q[:, 128 * r:128 * (r + 1)], jnp.zeros((), BF16)))
    qbd = jnp.concatenate(parts, axis=0)
    t_of_row = _iota((rows, 1), 0) % t_s
    jp = _iota((rows, n_w), 1)
    mask_p = (jp > t_of_row + (n_w - WINDOW)) & (jp <= t_of_row + n_w)
    jn = _iota((rows, PAGE), 1)
    mask_n = jn <= t_of_row
    s_p = jnp.where(mask_p, _dot(qbd, st_ref[0, 0:128, :].astype(BF16)), NEG_INF)
    s_n = jnp.where(mask_n, _dot(qbd, new_ref[0, 0:128, :].astype(BF16)), NEG_INF)
    mx = jnp.maximum(jnp.max(s_p, axis=-1, keepdims=True), jnp.max(s_n, axis=-1, keepdims=True))
    p_p = jnp.exp(s_p - mx)
    p_n = jnp.exp(s_n - mx)
    den = jnp.sum(p_p, axis=-1, keepdims=True) + jnp.sum(p_n, axis=-1, keepdims=True)
    o = (_dot_nt(p_p.astype(BF16), st_ref[0, 128:256, :].astype(BF16))
         + _dot_nt(p_n.astype(BF16), new_ref[0, 128:256, :].astype(BF16))) / den
    for r in range(GQA_R):
        o_ref[:, 128 * r:128 * (r + 1)] = jnp.where(
            lane_lo, o[t_s * r:t_s * (r + 1), :], o[t_s * (GQA_R + r):t_s * (GQA_R + r + 1), :])


def _win_s(nsaq, state_t, layer, newkv_t, bs):
    t_s = nsaq.shape[0] // bs
    n_w = state_t.shape[2]
    return pl.pallas_call(
        _win_s_kernel,
        grid=(bs,),
        in_specs=[pl.BlockSpec((t_s, 512), lambda b: (b, 0)),
                  pl.BlockSpec((1, 256, n_w), lambda b: (layer * bs + b, 0, 0)),
                  pl.BlockSpec((1, 256, PAGE), lambda b: (b, 0, 0))],
        out_specs=pl.BlockSpec((t_s, 512), lambda b: (b, 0)),
        out_shape=jax.ShapeDtypeStruct((bs * t_s, 512), F32),
        compiler_params=_cparams(("parallel",)),
        name="win_s",
    )(nsaq, state_t, newkv_t)


def _merge_kernel(x_ref, g_ref, osb_ref, ofox_ref, ocmp_ref, osel_ref, owin_ref, misc_ref,
                  wmg_ref, wsb_ref, wfox_ref, wnsa_ref, wo_ref, ex_ref, y_ref):
    x = x_ref[...]
    h = x * lax.rsqrt(jnp.mean(x * x, axis=-1, keepdims=True) + EPS) * g_ref[...]
    hb = h.astype(BF16)
    mhi, mlo = _split2(misc_ref[...])

    def gate(i):
        ex = ex_ref[:, 512 * i:512 * (i + 1)]
        return _dot(mhi, ex) + _dot(mlo, ex)

    o_nsa = gate(0) * ocmp_ref[...] + gate(1) * osel_ref[...] + gate(2) * owin_ref[...]
    d = x.shape[1]

    def mg(i):
        return jax.nn.sigmoid(_dot(hb, wmg_ref[:, d * i:d * (i + 1)]))

    mixed = (mg(0) * _dot(osb_ref[...].astype(BF16), wsb_ref[...])
             + mg(1) * _dot(ofox_ref[...].astype(BF16), wfox_ref[...])
             + mg(2) * _dot(o_nsa.astype(BF16), wnsa_ref[...]))
    y_ref[...] = x + _dot(mixed.astype(BF16), wo_ref[...])


def _merge(x2, o_sb, o_fox, o_cmp, o_sel, o_win, misc, lw, tm):
    n, d = x2.shape
    row = lambda i: (i, 0)
    const = lambda i: (0, 0)
    return pl.pallas_call(
        _merge_kernel,
        grid=(n // tm,),
        in_specs=[pl.BlockSpec((tm, d), row), pl.BlockSpec((1, d), const),
                  pl.BlockSpec((tm, 256), row), pl.BlockSpec((tm, 256), row),
                  pl.BlockSpec((tm, 512), row), pl.BlockSpec((tm, 512), row), pl.BlockSpec((tm, 512), row),
                  pl.BlockSpec((tm, 128), row),
                  pl.BlockSpec((d, N_MG), const), pl.BlockSpec((256, d), const), pl.BlockSpec((256, d), const),
                  pl.BlockSpec((512, d), const), pl.BlockSpec((d, d), const), pl.BlockSpec((128, 1536), const)],
        out_specs=pl.BlockSpec((tm, d), row),
        out_shape=jax.ShapeDtypeStruct((n, d), F32),
        compiler_params=_cparams(("parallel",)),
        name="merge",
    )(x2, lw["attn_norm"], o_sb, o_fox, o_cmp, o_sel, o_win, misc,
      lw["w_mg"], lw["w_br_sb"], lw["w_br_fox"], lw["w_br_nsa"], lw["w_o"], lw["ex"])


FF_CHUNK = 768


def _ffn_kernel(x_ref, g_ref, hist_ref, wup_ref, cw_ref, cb_ref, wdn_ref, y_ref, cs_ref, prev_ref):
    tm = x_ref.shape[0]
    d_ff = wdn_ref.shape[0]
    ti = pl.program_id(1)

    @pl.when(ti == 0)
    def _hist():
        prev_ref[...] = hist_ref[0]

    x = x_ref[...]
    h = x * lax.rsqrt(jnp.mean(x * x, axis=-1, keepdims=True) + EPS) * g_ref[...]
    hb = h.astype(BF16)
    rowi = _iota((tm, 1), 0)
    y = x
    for c in range(0, d_ff, FF_CHUNK):
        gate = _dot(hb, wup_ref[:, c:c + FF_CHUNK])
        up = _dot(hb, wup_ref[:, d_ff + c:d_ff + c + FF_CHUNK])
        p6 = prev_ref[6:7, c:c + FF_CHUNK]
        p7 = prev_ref[7:8, c:c + FF_CHUNK]
        g1 = jnp.where(rowi == 0, p7, pltpu.roll(gate, 1, 0))
        g2 = jnp.where(rowi == 0, p6, jnp.where(rowi == 1, p7, pltpu.roll(gate, 2, 0)))
        conv = (cb_ref[:, c:c + FF_CHUNK] + g2 * cw_ref[0:1, c:c + FF_CHUNK]
                + g1 * cw_ref[1:2, c:c + FF_CHUNK] + gate * cw_ref[2:3, c:c + FF_CHUNK])
        act = conv * jax.nn.sigmoid(conv) * up
        y = y + _dot(act.astype(BF16), wdn_ref[c:c + FF_CHUNK, :])
        tail = gate[tm - 8:tm, :]
        prev_ref[:, c:c + FF_CHUNK] = tail
        cs_ref[0, :, c:c + FF_CHUNK] = tail
    y_ref[...] = y


def _ffn(x2, hist, lw, nb, tm):
    n, d = x2.shape
    nt = n // nb // tm
    d_ff = lw["w_down"].shape[0]
    row = lambda b, i: (b * nt + i, 0)
    const = lambda b, i: (0, 0)
    hmap = (lambda b, i: (b, 0, 0)) if hist.shape[0] == nb else (lambda b, i: (0, 0, 0))
    once = dict(pipeline_mode=pl.Buffered(1))
    return pl.pallas_call(
        _ffn_kernel,
        grid=(nb, nt),
        in_specs=[pl.BlockSpec((tm, d), row), pl.BlockSpec((1, d), const), pl.BlockSpec((1, 8, d_ff), hmap),
                  pl.BlockSpec((d, 2 * d_ff), const, **once), pl.BlockSpec((CONV_W, d_ff), const),
                  pl.BlockSpec((1, d_ff), const), pl.BlockSpec((d_ff, d), const, **once)],
        out_specs=[pl.BlockSpec((tm, d), row), pl.BlockSpec((1, 8, d_ff), lambda b, i: (b, 0, 0))],
        out_shape=[jax.ShapeDtypeStruct((n, d), F32), jax.ShapeDtypeStruct((nb, 8, d_ff), F32)],
        scratch_shapes=[pltpu.VMEM((8, d_ff), F32)],
        compiler_params=_cparams(("parallel", "arbitrary")),
        name="ffn",
    )(x2, lw["ffn_norm"], hist, lw["w_up"], lw["conv_w"], lw["conv_b"], lw["w_down"])


def _consts():
    bd = (np.arange(256)[:, None] // HEAD_DIM == np.arange(256)[None, :] // HEAD_DIM)
    tri_rows = np.arange(256)[:, None] < np.arange(256)[None, :]
    tri128 = np.arange(PAGE)[:, None] > np.arange(PAGE)[None, :]
    hd, ps = np.arange(512) // 128, np.arange(512) % 128
    same = hd[:, None] == hd[None, :]
    c_r = same & (ps[:, None] > ps[None, :])
    c_w = same & (ps[:, None] <= ps[None, :])
    cmat = np.concatenate([c_r, c_w, same], axis=1)
    ex = np.zeros((128, 1536), np.float32)
    for br in range(3):
        for g in range(HKV_NSA):
            for r in range(GQA_R):
                m = 2 * r + g
                ex[H_FOX + br * 8 + g * GQA_R + r, 512 * br + 64 * m:512 * br + 64 * (m + 1)] = 1.0
    as_bf = lambda a: jnp.asarray(a.astype(np.float32), BF16)
    return dict(bd=as_bf(bd), tri_rows=as_bf(tri_rows), tri128=as_bf(tri128), cmat=as_bf(cmat), ex=as_bf(ex))


def _pool_matrices(pool):
    n_pos = G_PAGES * PAGE
    pos = np.arange(n_pos)
    blk = pos // CMP_BLOCK
    col = blk // 2 + (n_pos // SEL_BLOCK) * (blk % 2)
    place = jnp.asarray((col[:, None] == np.arange(128)[None, :]).astype(np.float32))
    mats = []
    for kv in range(2):
        for g in range(HKV_NSA):
            w = pool[kv, :, g][pos % CMP_BLOCK]
            mats.append(place * w[:, None])
    w = jnp.stack(mats, axis=0)
    hi = w.astype(BF16)
    lo = (w - hi.astype(F32)).astype(BF16)
    return hi, lo


def _layer_weights(l, cst, attn_norm, w_in, fox_forget_bias, fox_qk_gain, nsa_qk_gain, nsa_cmp_pool,
                   w_br_sb, w_br_fox, w_br_nsa, w_o, ffn_norm, w_up, conv_w, conv_b, w_down):
    w = w_in[l]
    d = w.shape[0]
    o_foxf = 6 * 256
    o_nsaq = o_foxf + H_FOX
    o_kv = o_nsaq + H_NSA * HEAD_DIM
    o_gate = o_kv + 6 * 128
    o_mg = o_gate + 3 * H_NSA
    head_order = [(m % 2) * GQA_R + m // 2 for m in range(H_NSA)]
    nsaq_cols = jnp.concatenate([w[:, o_nsaq + 64 * hh:o_nsaq + 64 * (hh + 1)] for hh in head_order], axis=1)
    misc = jnp.concatenate([w[:, o_foxf:o_foxf + H_FOX], w[:, o_gate:o_gate + 3 * H_NSA],
                            jnp.zeros((d, 128 - H_FOX - 3 * H_NSA), w.dtype)], axis=1)
    w_a = jnp.concatenate([w[:, 0:o_foxf], nsaq_cols, w[:, o_kv:o_gate], misc], axis=1).astype(BF16)
    tile = lambda v, n: jnp.tile(v, n)[None, :]
    gains = jnp.concatenate([
        tile(fox_qk_gain[l, 0], 4), tile(fox_qk_gain[l, 1], 4), tile(nsa_qk_gain[l, 0], 4),
        tile(nsa_qk_gain[l, 1], 4), tile(nsa_qk_gain[l, 2], 4), tile(nsa_qk_gain[l, 3], 4),
        jnp.zeros((2, 256), F32)], axis=0)
    fbias = jnp.concatenate([fox_forget_bias[l], jnp.zeros((128 - H_FOX,), F32)])[None, :]
    pool = nsa_cmp_pool[l]
    lanes = lambda p: jnp.repeat(p, HEAD_DIM, axis=1)
    pw = jnp.concatenate([lanes(pool[0]), lanes(pool[1])], axis=1)
    zero = jnp.zeros_like(pw)
    wph, wpl = _pool_matrices(pool)
    nsa_rows = jnp.concatenate([w_br_nsa[l, 64 * hh:64 * (hh + 1)] for hh in head_order], axis=0)
    return dict(
        attn_norm=attn_norm[l][None, :], w_a=w_a, w_mg=w[:, o_mg:].astype(BF16), bd=cst["bd"], gains=gains,
        fbias=fbias, pwe=jnp.concatenate([pw, zero], axis=0), pwo=jnp.concatenate([zero, pw], axis=0),
        wph=wph, wpl=wpl,
        w_br_sb=w_br_sb[l].astype(BF16), w_br_fox=w_br_fox[l].astype(BF16), w_br_nsa=nsa_rows.astype(BF16),
        w_o=w_o[l].astype(BF16), ex=cst["ex"], ffn_norm=ffn_norm[l][None, :], w_up=w_up[l].astype(BF16),
        conv_w=conv_w[l], conv_b=conv_b[l][None, :], w_down=w_down[l].astype(BF16))


def _rope_tables(pos):
    half = ROPE_DIM // 2
    inv = ROPE_THETA ** (-jnp.arange(half, dtype=F32) * (2.0 / ROPE_DIM))
    ang = pos.astype(F32)[:, None] * inv[None, :]
    cos, sin = jnp.cos(ang), jnp.sin(ang)
    t = pos.shape[0]
    one = jnp.ones((t, HEAD_DIM - ROPE_DIM), F32)
    zero8 = jnp.zeros((t, half), F32)
    zero48 = jnp.zeros((t, HEAD_DIM - ROPE_DIM), F32)
    two = lambda a: jnp.concatenate([a, a], axis=1)
    return (two(jnp.concatenate([cos, cos, one], axis=1)),
            two(jnp.concatenate([-sin, zero8, zero48], axis=1)),
            two(jnp.concatenate([zero8, sin, zero48], axis=1)))


def _new_block_t(a, bs):
    t = a.shape[0] // bs
    at = jnp.swapaxes(a.reshape(bs, t, a.shape[1]), 1, 2)
    return jnp.pad(at, ((0, 0), (0, 0), (0, PAGE - t)))


def _feature_major(c):
    depth, n, npos = c.shape[0:3]
    ct = jnp.moveaxis(c, 2, -1)
    return ct.reshape(depth * n, -1, npos)


def _prompt_layer(x2, b, t, lw, cst, tabs):
    tm = 512 if t % 512 == 0 else 256
    (sbk, foxk, nsak, nsaq, cmpkv, misc,
     sbqt, foxqt, nsaqt, sbkvt, foxkvt, cmpt, selt, wint, lft) = _proj(x2, lw, *tabs, tm, nb=b)
    o_sb = _attn("sb", sbqt, sbk, sbkvt, b, t, tri=cst["tri_rows"])
    o_fox = _attn("fox", foxqt, foxk, foxkvt, b, t, extra=misc, tri=cst["tri_rows"])
    o_cmp, selm = _cmp_p(nsaq, cmpkv, lw, b, t)
    o_sel = _attn("sel", nsaqt, nsak, selt, b, t, kcol=1, extra=selm)
    o_win = _attn("win", nsaqt, nsak, wint, b, t, kcol=2)
    x_mid = _merge(x2, o_sb, o_fox, o_cmp, o_sel, o_win, misc, lw, 256)
    d_ff = lw["w_down"].shape[0]
    y, cs = _ffn(x_mid, jnp.zeros((1, 8, d_ff), F32), lw, b, 256)
    n_w = min(WINDOW, t)
    state = (sbkvt, foxkvt, lft[:, 0:H_FOX], cmpt, selt, wint[:, :, t - n_w:], cs[:, 8 - (CONV_W - 1):])
    return y, state


def _sample_layer(x2, bs, l, lw, cst, tabs, page_tab, c_sb, c_fox, c_logf, c_cmp, c_sel, st_win, st_conv):
    n = x2.shape[0]
    t_s = n // bs
    sbq, sbkv, foxq, foxkv, nsaq, cmpkv, selkv, winkv, misc = _proj(x2, lw, *tabs, n)
    logf_new = misc[:, 0:H_FOX]
    lf_new_pages = _new_block_t(logf_new, bs).reshape(bs, PAGE * H_FOX)
    rt_new = _logf_sums(lf_new_pages, cst["cmat"]).reshape(bs, 1, 1536)
    n_pool = c_logf.shape[0] // 2
    rt_cache = _logf_sums(c_logf[l * n_pool:(l + 1) * n_pool].reshape(n_pool, PAGE * H_FOX), cst["cmat"])
    rt_cache = rt_cache.reshape(n_pool, 1, 1536)
    o_sb = _dec("sb", sbq, _new_block_t(sbkv, bs), c_sb, l, page_tab, cst["tri128"])
    o_fox = _dec("fox", foxq, _new_block_t(foxkv, bs), c_fox, l, page_tab, rt_new=rt_new, rt_cache=rt_cache)
    o_cmp, selm = _cmp_s(nsaq, c_cmp, l, page_tab, lw)
    o_sel = _dec("sel", nsaq, _new_block_t(selkv, bs), c_sel, l, page_tab, selm)
    o_win = _win_s(nsaq, st_win, l, _new_block_t(winkv, bs), bs)
    x_mid = _merge(x2, o_sb, o_fox, o_cmp, o_sel, o_win, misc, lw, n)
    hist = jnp.pad(st_conv, ((0, 0), (8 - (CONV_W - 1), 0), (0, 0)))
    y, cs = _ffn(x_mid, hist, lw, bs, t_s)
    state = (sbkv, foxkv, logf_new, cmpkv, selkv, winkv, cs[:, 8 - (CONV_W - 1):])
    return y, state


def kernel(x_prompt, x_sample, cache_sb_kv, cache_fox_kv, cache_fox_logf, cache_nsa_cmp_kv, cache_nsa_sel_kv, state_nsa_win_kv, state_ffn_conv, page_table, attn_norm, w_in, fox_forget_bias, fox_qk_gain, nsa_qk_gain, nsa_cmp_pool, w_br_sb, w_br_fox, w_br_nsa, w_o, ffn_norm, w_up, conv_w, conv_b, w_down):
    b, t, d = x_prompt.shape
    bs, t_s, _ = x_sample.shape
    depth = w_in.shape[0]
    n_pages = page_table.shape[1]
    past = n_pages * PAGE
    n_w = state_nsa_win_kv.shape[2]
    assert t % 256 == 0 and t_s == 8 and n_pages % G_PAGES == 0 and past >= WINDOW and depth == 2
    cst = _consts()
    tabs_p = _rope_tables(jnp.arange(t, dtype=jnp.int32))
    tabs_s = tuple(jnp.tile(a, (bs, 1)) for a in _rope_tables(past + jnp.arange(t_s, dtype=jnp.int32)))
    c_sb, c_fox = _feature_major(cache_sb_kv), _feature_major(cache_fox_kv)
    c_cmp, c_sel = _feature_major(cache_nsa_cmp_kv), _feature_major(cache_nsa_sel_kv)
    c_logf = _feature_major(cache_fox_logf)
    st_win = _feature_major(state_nsa_win_kv)
    hp = x_prompt.reshape(b * t, d)
    hs = x_sample.reshape(bs * t_s, d)
    st_p, st_s = [], []
    for l in range(depth):
        lw = _layer_weights(l, cst, attn_norm, w_in, fox_forget_bias, fox_qk_gain, nsa_qk_gain, nsa_cmp_pool,
                            w_br_sb, w_br_fox, w_br_nsa, w_o, ffn_norm, w_up, conv_w, conv_b, w_down)
        hp, new_p = _prompt_layer(hp, b, t, lw, cst, tabs_p)
        hs, new_s = _sample_layer(hs, bs, l, lw, cst, tabs_s, page_table, c_sb, c_fox, c_logf, c_cmp, c_sel,
                                  st_win, state_ffn_conv[l])
        st_p.append(new_p)
        st_s.append(new_s)

    def stack_p(i, shape):
        a = jnp.stack([s[i] for s in st_p], axis=0)
        a = a.reshape((depth, b) + shape + (a.shape[-1],))
        return jnp.moveaxis(a, -1, 2)

    def stack_s(i, shape):
        return jnp.stack([s[i].reshape((bs, t_s) + shape) for s in st_s], axis=0)

    kv = lambda h: (2, h, HEAD_DIM)
    win_new = stack_s(5, kv(HKV_NSA))
    win_s = jnp.concatenate([state_nsa_win_kv[:, :, t_s:], win_new], axis=2)[:, :, -n_w:]
    return (hp.reshape(b, t, d), hs.reshape(bs, t_s, d),
            stack_p(0, kv(H_SB)), stack_s(0, kv(H_SB)),
            stack_p(1, kv(H_FOX)), stack_s(1, kv(H_FOX)),
            stack_p(2, (H_FOX,)), stack_s(2, (H_FOX,)),
            stack_p(3, kv(HKV_NSA)), stack_s(3, kv(HKV_NSA)),
            stack_p(4, kv(HKV_NSA)), stack_s(4, kv(HKV_NSA)),
            stack_p(5, kv(HKV_NSA)), win_s,
            jnp.stack([s[6] for s in st_p], axis=0), jnp.stack([s[6] for s in st_s], axis=0))
```

```python
import functools

import numpy as np
import jax
import jax.numpy as jnp
from jax import lax
from jax.experimental import pallas as pl
from jax.experimental.pallas import tpu as pltpu

F32 = jnp.float32
BF16 = jnp.bfloat16

HEAD_DIM = 64
H_SB = 4
H_FOX = 4
H_NSA = 8
HKV_NSA = 2
GQA_R = H_NSA // HKV_NSA
ROPE_DIM = HEAD_DIM // 4
ROPE_THETA = 500000.0
CMP_BLOCK = 32
SEL_BLOCK = 64
SEL_TOPK = 16
WINDOW = 512
CONV_W = 3
EPS = 1e-6
SCALE = HEAD_DIM ** -0.5
NEG_INF = -1e30
TINY = 1e-30
FORCE_SCORE = 1e6
PAGE = 128
LANES = 128
VMEM_LIMIT = 56 * 1024 * 1024
SB_DEAD = 110.0

C_SBQ, C_SBKV, C_FOXQ, C_FOXK, C_FOXV = 0, 256, 768, 1024, 1280
C_NSAQ, C_CMP, C_SEL, C_WIN, C_MISC, C_END = 1536, 2048, 2304, 2560, 2816, 2944
N_MG = 3 * 1024


def _dot(a, b):
    return jnp.dot(a, b, preferred_element_type=F32)


def _dot_nt(a, b):
    return lax.dot_general(a, b, (((1,), (1,)), ((), ())), preferred_element_type=F32)


def _split2(x):
    hi = x.astype(BF16)
    lo = (x - hi.astype(F32)).astype(BF16)
    return hi, lo


def _split3(x):
    hi = x.astype(BF16)
    r = x - hi.astype(F32)
    mid = r.astype(BF16)
    lo = (r - mid.astype(F32)).astype(BF16)
    return hi, mid, lo


def _softplus(z):
    return jnp.maximum(z, 0.0) + jnp.log(1.0 + jnp.exp(-jnp.abs(z)))


def _iota(shape, dim):
    return lax.broadcasted_iota(jnp.int32, shape, dim)


def _cparams(sem):
    return pltpu.CompilerParams(dimension_semantics=sem, vmem_limit_bytes=VMEM_LIMIT)


def _proj_kernel(feature_major, x_ref, g_ref, w_ref, bd_ref, gains_ref, cos_ref, sa_ref, sb_ref, fb_ref, *outs):
    x = x_ref[...]
    h = x * lax.rsqrt(jnp.mean(x * x, axis=-1, keepdims=True) + EPS) * g_ref[...]
    hb = h.astype(BF16)

    def seg(lo, n):
        return _dot_nt(hb, w_ref[lo:lo + n, :])

    def hrms(z, gain):
        w = z.shape[1]
        hi, lo = _split2(z * z)
        bd = bd_ref[0:w, 0:w]
        ss = _dot(hi, bd) + _dot(lo, bd)
        return z * lax.rsqrt(ss * (1.0 / HEAD_DIM) + EPS) * gain

    def rope(z):
        return (z * cos_ref[...] + pltpu.roll(z, LANES - ROPE_DIM // 2, 1) * sa_ref[...]
                + pltpu.roll(z, ROPE_DIM // 2, 1) * sb_ref[...])

    sb_q = seg(C_SBQ, 256) * SCALE
    sb_kv = seg(C_SBKV, 512)
    fox_q = hrms(seg(C_FOXQ, 256), gains_ref[0:1, :]) * SCALE
    fox_k = hrms(seg(C_FOXK, 256), gains_ref[1:2, :])
    fox_v = seg(C_FOXV, 256)
    nsa_q = []
    for c in range(2):
        zq = hrms(seg(C_NSAQ + 256 * c, 256), gains_ref[2:3, :])
        nsa_q += [rope(zq[:, 128 * s:128 * (s + 1)]) * SCALE for s in range(2)]
    nsa_k, nsa_v = [], []
    for i, col in enumerate((C_CMP, C_SEL, C_WIN)):
        nsa_k.append(rope(hrms(seg(col, 128), gains_ref[3 + i:4 + i, 0:128])))
        nsa_v.append(seg(col + 128, 128))
    zm = seg(C_MISC, 128)
    lane = _iota((1, LANES), 1)
    misc = jnp.where(lane < H_FOX, -_softplus(-(zm + fb_ref[...])), jax.nn.sigmoid(zm))

    if not feature_major:
        sbq_ref, sbkv_ref, foxq_ref, foxkv_ref, nsaq_ref, cmp_ref, sel_ref, win_ref, misc_ref = outs
        sbq_ref[...] = sb_q.astype(BF16)
        sbkv_ref[...] = sb_kv
        foxq_ref[...] = fox_q.astype(BF16)
        foxkv_ref[:, 0:256] = fox_k
        foxkv_ref[:, 256:512] = fox_v
        for s in range(4):
            nsaq_ref[:, 128 * s:128 * (s + 1)] = nsa_q[s].astype(BF16)
        for i, ref in enumerate((cmp_ref, sel_ref, win_ref)):
            ref[:, 0:128] = nsa_k[i]
            ref[:, 128:256] = nsa_v[i]
        misc_ref[...] = misc
        return

    (sbk_ref, foxk_ref, nsak_ref, nsaq_ref, cmpkv_ref, misc_ref,
     sbqt_ref, foxqt_ref, nsaqt_ref, sbkvt_ref, foxkvt_ref, cmpt_ref, selt_ref, wint_ref, lft_ref) = outs
    sbk_ref[...] = sb_kv[:, 0:256].astype(BF16)
    foxk_ref[...] = fox_k.astype(BF16)
    for i in range(3):
        nsak_ref[:, 128 * i:128 * (i + 1)] = nsa_k[i].astype(BF16)
    for s in range(4):
        nsaq_ref[:, 128 * s:128 * (s + 1)] = nsa_q[s].astype(BF16)
        nsaqt_ref[0, 128 * s:128 * (s + 1), :] = nsa_q[s].T.astype(BF16)
    cmpkv_ref[:, 0:128] = nsa_k[0]
    cmpkv_ref[:, 128:256] = nsa_v[0]
    misc_ref[...] = misc
    sbqt_ref[0] = sb_q.T.astype(BF16)
    foxqt_ref[0] = fox_q.T.astype(BF16)
    sbkvt_ref[0] = sb_kv.T
    foxkvt_ref[0, 0:256, :] = fox_k.T
    foxkvt_ref[0, 256:512, :] = fox_v.T
    for i, ref in enumerate((cmpt_ref, selt_ref, wint_ref)):
        ref[0, 0:128, :] = nsa_k[i].T
        ref[0, 128:256, :] = nsa_v[i].T
    lft_ref[0] = misc.T[0:8, :]


def _proj(x2, lw, cos_t, sa_t, sb_t, tm, nb=None):
    n, d = x2.shape
    nt = cos_t.shape[0] // tm
    row = lambda i: (i, 0)
    const = lambda i: (0, 0)
    tab = pl.BlockSpec((tm, LANES), lambda i: (i % nt, 0))
    if nb is None:
        outs = [(256, BF16), (512, F32), (256, BF16), (512, F32), (512, BF16),
                (256, F32), (256, F32), (256, F32), (128, F32)]
        out_specs = [pl.BlockSpec((tm, w), row) for w, _ in outs]
        out_shape = [jax.ShapeDtypeStruct((n, w), dt) for w, dt in outs]
    else:
        t = n // nb
        nat = [(256, BF16), (256, BF16), (384, BF16), (512, BF16), (256, F32), (128, F32)]
        fm = [(256, BF16), (256, BF16), (512, BF16), (512, F32), (512, F32), (256, F32), (256, F32), (256, F32),
              (8, F32)]
        fmap = lambda i: (i // nt, 0, i % nt)
        out_specs = ([pl.BlockSpec((tm, w), row) for w, _ in nat]
                     + [pl.BlockSpec((1, w, tm), fmap) for w, _ in fm])
        out_shape = ([jax.ShapeDtypeStruct((n, w), dt) for w, dt in nat]
                     + [jax.ShapeDtypeStruct((nb, w, t), dt) for w, dt in fm])
    return pl.pallas_call(
        functools.partial(_proj_kernel, nb is not None),
        grid=(n // tm,),
        in_specs=[pl.BlockSpec((tm, d), row), pl.BlockSpec((1, d), const),
                  pl.BlockSpec((C_END, d), const), pl.BlockSpec((256, 256), const),
                  pl.BlockSpec((8, 256), const), tab, tab, tab, pl.BlockSpec((1, LANES), const)],
        out_specs=out_specs,
        out_shape=out_shape,
        compiler_params=_cparams(("parallel",)),
        name="proj",
    )(x2, lw["attn_norm"], lw["w_a"], lw["bd"], lw["gains"], cos_t, sa_t, sb_t, lw["fbias"])


def _logf_kernel(x_ref, c_ref, o_ref):
    parts = _split3(x_ref[...])
    c = c_ref[...]
    o_ref[...] = _dot(parts[0], c) + _dot(parts[1], c) + _dot(parts[2], c)


def _logf_sums(lf_pages, cmat):
    p = lf_pages.shape[0]
    tp = 256 if p % 256 == 0 else p
    return pl.pallas_call(
        _logf_kernel,
        grid=(p // tp,),
        in_specs=[pl.BlockSpec((tp, 512), lambda i: (i, 0)), pl.BlockSpec((512, 1536), lambda i: (0, 0))],
        out_specs=pl.BlockSpec((tp, 1536), lambda i: (i, 0)),
        out_shape=jax.ShapeDtypeStruct((p, 1536), F32),
        compiler_params=_cparams(("parallel",)),
        name="logf",
    )(lf_pages, cmat)


def _fox_bias_kernel(lf_ref, tri_ref, aug_ref, tt_ref):
    tk = lf_ref.shape[0]
    lf = lf_ref[...]
    tri = tri_ref[...]
    p3 = _split3(lf)
    rb = _dot(tri, p3[0]) + _dot(tri, p3[1]) + _dot(tri, p3[2])
    hi = rb.astype(BF16).astype(F32)
    r1 = rb - hi
    mid = r1.astype(BF16).astype(F32)
    lo = (r1 - mid).astype(BF16).astype(F32)
    li = _iota((tk, LANES), 1)
    comb = jnp.where(li < 20, pltpu.roll(hi, 16, 1), jnp.where(li < 24, pltpu.roll(mid, 20, 1), pltpu.roll(lo, 24, 1)))
    comb = jnp.where((li >= 16) & (li < 28), comb, 0.0)
    aug_ref[...] = jnp.where(li < 3, 1.0, comb).astype(BF16)
    tot = rb[0:1, :] + lf[0:1, :]
    diag = _iota((8, LANES), 0) == _iota((8, LANES), 1)
    tt_ref[0] = jnp.broadcast_to(jnp.sum(jnp.where(diag, tot, 0.0), axis=1, keepdims=True), (8, tk))


def _fox_bias(misc, tri, tk=256):
    n = misc.shape[0]
    return pl.pallas_call(
        _fox_bias_kernel,
        grid=(n // tk,),
        in_specs=[pl.BlockSpec((tk, 128), lambda i: (i, 0)), pl.BlockSpec((tk, tk), lambda i: (0, 0))],
        out_specs=[pl.BlockSpec((tk, 128), lambda i: (i, 0)), pl.BlockSpec((1, 8, tk), lambda i: (i, 0, 0))],
        out_shape=[jax.ShapeDtypeStruct((n, 128), BF16), jax.ShapeDtypeStruct((n // tk, 8, tk), F32)],
        compiler_params=_cparams(("parallel",)),
        name="fox_bias",
    )(misc, tri)


def _attn_kernel(mode, nsa, tq, tk, n_s, qi_ref, kb_ref, *refs):
    refs = list(refs)
    qt_ref, k_ref, vt_ref = refs[0:3]
    pos = 3
    tri_ref = lf_ref = tt_ref = sm_ref = None
    if mode == "sb":
        tri_ref = refs[pos]; pos += 1
    if mode == "fox":
        lf_ref, tt_ref = refs[pos], refs[pos + 1]; pos += 2
    if mode == "sel":
        sm_ref = refs[pos]; pos += 1
    o_ref, acc_ref, m_ref = refs[pos:pos + 3]
    cf_ref = refs[pos + 3] if mode == "fox" else None
    dead_ref = refs[pos + 3] if mode == "sb" else None
    nh = H_NSA if nsa else H_SB

    p_id = pl.program_id(1)
    qi = qi_ref[p_id]
    kb = kb_ref[p_id]
    first = kb == qi
    w_blocks = WINDOW // tk
    last = (kb == jnp.maximum(qi - w_blocks, 0)) if mode == "win" else (kb == 0)

    @pl.when(first)
    def _init():
        acc_ref[...] = jnp.zeros_like(acc_ref)
        m_ref[...] = jnp.zeros_like(m_ref) if mode == "sb" else jnp.full_like(m_ref, NEG_INF)
        if mode == "fox":
            cf_ref[...] = jnp.zeros_like(cf_ref)
        if mode == "sb":
            dead_ref[0] = 0

    row_lo = _iota((LANES, 1), 0) < HEAD_DIM

    def body(edge):
        mask = None
        if edge:
            d = (qi * tq + _iota((tk, tq), 1)) - (kb * tk + _iota((tk, tq), 0))
            if mode == "sb":
                mask = d > 0
            elif mode == "win":
                mask = (d >= 0) & (d < WINDOW)
            else:
                mask = d >= 0
        cf = None
        if mode == "fox":
            cf = cf_ref[...]
            cf_ref[...] = cf + tt_ref[0]

        hms, vts, qtms, kss = [], [], [], []
        for n in range(nh):
            slab, sub = divmod(n, 2)
            ko = 0 if nsa else 128 * slab
            hm = row_lo if sub == 0 else jnp.logical_not(row_lo)
            qtms.append(jnp.where(hm, qt_ref[128 * slab:128 * (slab + 1), :], jnp.zeros((), BF16)))
            kss.append(k_ref[:, ko:ko + 128])
            vo = 0 if nsa else 128 * slab
            vts.append(vt_ref[0, 0, vo:vo + 128, :].astype(BF16))
            hms.append(hm)
        if mode == "sb":
            ss = [_dot(kss[n], qtms[n]) for n in range(nh)]
            tri = tri_ref[...]
            sps, spms, hls = [], [], []
            for n in range(nh):
                sp = _softplus(ss[n])
                spm = sp if mask is None else jnp.where(mask, sp, 0.0)
                sps.append(sp)
                spms.append(spm)
                hls.append(_split2(spm))
            als = [_dot(tri, hi) + _dot(tri, lo) for hi, lo in hls]
            avs = []
            for n in range(nh):
                carry = m_ref[n]
                a = jnp.exp((ss[n] - sps[n]) - (als[n] + carry[0:1, :]))
                if mask is not None:
                    a = jnp.where(mask, a, 0.0)
                avs.append(a.astype(BF16))
                m_ref[n] = carry + (als[n][0:1, :] + spms[n][0:1, :])
            for n in range(nh):
                acc_ref[n] += _dot(vts[n], avs[n])
            dead_ref[0] = (jnp.min(m_ref[...]) > SB_DEAD).astype(jnp.int32)
        else:
            li = _iota((tk, LANES), 1)
            ones3 = li < 3
            ri16 = _iota((16, 1), 0)
            zeros16 = jnp.zeros((16, tq), BF16)
            if mode == "sel":
                smb = ((sm_ref[0].astype(F32) - 1.0) * 1e30).astype(BF16)
                blk = 16 + kb * (tk // SEL_BLOCK) + _iota((tk, LANES), 0) // SEL_BLOCK
                augl = jnp.where(ones3 | (li == blk), 1.0, 0.0).astype(BF16)
                pad = jnp.zeros((LANES - 16 - n_s, tq), BF16)
                bases = [jnp.concatenate([smb[g * n_s:(g + 1) * n_s], pad], axis=0) for g in range(2)]
                base_of = lambda n: bases[n % 2]
            elif mode == "fox":
                augl = lf_ref[...]
                pad = jnp.zeros((LANES - 32, tq), BF16)
                pick = [jnp.broadcast_to(jnp.where(((ri16 % 4) == n) & (ri16 < 12), 1.0, 0.0), (16, tq))
                        .astype(BF16) for n in range(nh)]
                base_of = lambda n: jnp.concatenate([pick[n], pad], axis=0)
            else:
                augl = jnp.where(ones3, 1.0, 0.0).astype(BF16)
                base_win = jnp.zeros((LANES - 16, tq), BF16)
                base_of = lambda n: base_win
            lhs = [jnp.concatenate([kss[n], augl], axis=1) for n in range(nh)]
            m_news, alphas = [], []
            for n in range(nh):
                s = _dot(lhs[n], jnp.concatenate([qtms[n], zeros16, base_of(n)], axis=0))
                if mask is not None:
                    s = jnp.where(mask, s, NEG_INF)
                m_old = m_ref[n]
                m_blk = jnp.max(s, axis=0, keepdims=True)
                if mode == "fox":
                    m_blk = m_blk + cf[n:n + 1, :]
                m_new = jnp.maximum(m_old, m_blk)
                m_news.append(m_new)
                alphas.append(jnp.exp(m_old - m_new))
                m_ref[n] = m_new
            ps = []
            for n in range(nh):
                shift = -m_news[n][0:1, :]
                if mode == "fox":
                    shift = shift + cf[n:n + 1, :]
                t3 = _split3(shift)
                terms = jnp.where(ri16 == 0, t3[0].astype(F32), jnp.where(
                    ri16 == 1, t3[1].astype(F32), jnp.where(ri16 == 2, t3[2].astype(F32), 0.0))).astype(BF16)
                s = _dot(lhs[n], jnp.concatenate([qtms[n], terms, base_of(n)], axis=0))
                if mask is not None:
                    s = jnp.where(mask, s, NEG_INF)
                ps.append(jnp.exp(s).astype(BF16))
            for n in range(nh):
                vaug = jnp.where(hms[n], vts[n], jnp.ones((), BF16))
                acc_ref[n] = acc_ref[n] * alphas[n][0:1, :] + _dot(vaug, ps[n])

    if mode == "win":
        edge = first | (kb == qi - w_blocks)
    else:
        edge = first
    live = (dead_ref[0] == 0) if mode == "sb" else True
    @pl.when(edge & live)
    def _edge():
        body(True)

    @pl.when(jnp.logical_not(edge) & live)
    def _inner():
        body(False)

    @pl.when(last)
    def _fin():
        for slab in range(nh // 2):
            a0, a1 = acc_ref[2 * slab], acc_ref[2 * slab + 1]
            if mode == "sb":
                o0, o1 = a0[0:HEAD_DIM, :], a1[HEAD_DIM:LANES, :]
            else:
                o0 = a0[0:HEAD_DIM, :] / a0[HEAD_DIM:HEAD_DIM + 1, :]
                o1 = a1[HEAD_DIM:LANES, :] / a1[0:1, :]
            o_ref[:, 128 * slab:128 * (slab + 1)] = jnp.concatenate([o0, o1], axis=0).T


def _pairs(nq, mode, tk):
    qi, kb = [], []
    for i in range(nq):
        lo = max(i - WINDOW // tk, 0) if mode == "win" else 0
        for k in range(i, lo - 1, -1):
            qi.append(i)
            kb.append(k)
    return jnp.asarray(qi, jnp.int32), jnp.asarray(kb, jnp.int32)


def _attn(mode, qt, k, kvt, b, t, kcol=0, extra=None, tri=None, tq=256, tk=256):
    nsa = mode in ("sel", "win")
    wq = qt.shape[1]
    wv = kvt.shape[1] // 2
    wk = 128 if nsa else 256
    nq = t // tq
    n_s = t // SEL_BLOCK
    qi_tab, kb_tab = _pairs(nq, mode, tk)
    in_specs = [pl.BlockSpec((1, wq, tq), lambda bi, p, qi, kb: (bi, 0, qi[p])),
                pl.BlockSpec((tk, wk), lambda bi, p, qi, kb: (bi * (t // tk) + kb[p], kcol)),
                pl.BlockSpec((1, 1, wv, tk), lambda bi, p, qi, kb: (bi, 1, 0, kb[p]))]
    args = [qt, k, kvt.reshape(b, 2, wv, t)]
    if mode == "sb":
        in_specs.append(pl.BlockSpec((tk, tk), lambda bi, p, qi, kb: (0, 0)))
        args.append(tri)
    if mode == "fox":
        assert tq == tk
        in_specs.append(pl.BlockSpec((tk, 128), lambda bi, p, qi, kb: (bi * (t // tk) + kb[p], 0)))
        in_specs.append(pl.BlockSpec((1, 8, tk), lambda bi, p, qi, kb: (bi * (t // tk) + kb[p], 0, 0)))
        args += list(extra)
    if mode == "sel":
        in_specs.append(pl.BlockSpec((1, 2 * n_s, tq), lambda bi, p, qi, kb: (bi, 0, qi[p])))
        args.append(extra)
    nh = H_NSA if nsa else H_SB
    scratch = [pltpu.VMEM((nh, 128, tq), F32), pltpu.VMEM((nh, 8, tq), F32)]
    if mode == "fox":
        scratch.append(pltpu.VMEM((8, tq), F32))
    if mode == "sb":
        scratch.append(pltpu.SMEM((1,), jnp.int32))

    def kernel(qi_ref, kb_ref, qt_ref, *rest):
        _attn_kernel(mode, nsa, tq, tk, n_s, qi_ref, kb_ref, qt_ref.at[0], *rest)

    return pl.pallas_call(
        kernel,
        grid_spec=pltpu.PrefetchScalarGridSpec(
            num_scalar_prefetch=2, grid=(b, int(qi_tab.shape[0])),
            in_specs=in_specs, out_specs=pl.BlockSpec((tq, wq), lambda bi, p, qi, kb: (bi * nq + qi[p], 0)),
            scratch_shapes=scratch),
        out_shape=jax.ShapeDtypeStruct((b * t, wq), F32),
        compiler_params=_cparams(("parallel", "arbitrary")),
        name="attn_" + mode,
    )(qi_tab, kb_tab, *args)


def _pool_blocks(x, pwe_ref, pwo_ref):
    n = x.shape[0] // SEL_BLOCK
    x3 = x.reshape(n, SEL_BLOCK, 256)
    return (jnp.sum(x3 * pwe_ref[...][None], axis=1), jnp.sum(x3 * pwo_ref[...][None], axis=1))


def _rank_select(sc, n_blk, k_sel):
    j = _iota((n_blk, 1), 0)
    rank = jnp.zeros(sc.shape, F32)
    for i in range(n_blk):
        ri = sc[i:i + 1, :]
        beats = (ri > sc) | ((ri == sc) & (j > i))
        rank = rank + beats.astype(F32)
    return (rank < k_sel).astype(F32)


def _cmp_p_kernel(tq, t, q_ref, ckv_ref, pwe_ref, pwo_ref, o_ref, sm_ref, kc_ref):
    n_s = t // SEL_BLOCK
    n_c = 2 * n_s
    qi = pl.program_id(1)

    @pl.when(qi == 0)
    def _pool():
        ev, od = _pool_blocks(ckv_ref[...], pwe_ref, pwo_ref)
        kc_ref[0:n_s, :] = ev
        kc_ref[n_s:n_c, :] = od

    kcb = kc_ref[:, 0:128].astype(BF16)
    vcb = kc_ref[:, 128:256].astype(BF16)

    def c_end(i):
        blk = jnp.where(i < n_s, 2 * i, 2 * (i - n_s) + 1)
        return (blk + 1) * CMP_BLOCK - 1

    cmask = c_end(_iota((1, n_c), 1)) <= qi * tq + _iota((tq, 1), 0)
    t_row = qi * tq + _iota((1, tq), 1)
    cmask_t = c_end(_iota((n_c, 1), 0)) <= t_row
    lane_lo = _iota((1, LANES), 1) < HEAD_DIM
    imp = [jnp.zeros((n_s, tq), F32), jnp.zeros((n_s, tq), F32)]
    outs = {}
    for n in range(H_NSA):
        r, g = divmod(n, 2)
        hm = lane_lo if g == 0 else jnp.logical_not(lane_lo)
        qh = jnp.where(hm, q_ref[:, 128 * r:128 * (r + 1)], jnp.zeros((), BF16))
        s = jnp.where(cmask, _dot_nt(qh, kcb), NEG_INF)
        e = jnp.where(cmask, jnp.exp(s - jnp.max(s, axis=-1, keepdims=True)), 0.0)
        p = e / jnp.maximum(jnp.sum(e, axis=-1, keepdims=True), TINY)
        outs[(r, g)] = _dot(p.astype(BF16), vcb)
        st = jnp.where(cmask_t, _dot_nt(kcb, qh), NEG_INF)
        et = jnp.where(cmask_t, jnp.exp(st - jnp.max(st, axis=0, keepdims=True)), 0.0)
        pt = et / jnp.maximum(jnp.sum(et, axis=0, keepdims=True), TINY)
        imp[g] = imp[g] + (pt[0:n_s, :] + pt[n_s:n_c, :])
    for r in range(GQA_R):
        o_ref[:, 128 * r:128 * (r + 1)] = jnp.where(lane_lo, outs[(r, 0)], outs[(r, 1)])

    j = _iota((n_s, 1), 0)
    cur = t_row // SEL_BLOCK
    forced = (j == 0) | (j == cur) | (j == cur - 1)
    valid = j * SEL_BLOCK <= t_row
    sel = []
    for g in range(2):
        sc = jnp.where(forced, FORCE_SCORE, jnp.where(valid, imp[g], -1.0))
        sel.append(_rank_select(sc, n_s, min(SEL_TOPK, n_s)))
    sm_ref[0] = jnp.concatenate(sel, axis=0).astype(BF16)


def _cmp_p(nsaq, ckv, lw, b, t, tq=256):
    n = nsaq.shape[0]
    nq = t // tq
    n_s = t // SEL_BLOCK
    rowmap = lambda bi, qi: (bi * nq + qi, 0)
    const = lambda bi, qi: (0, 0)
    return pl.pallas_call(
        functools.partial(_cmp_p_kernel, tq, t),
        grid=(b, nq),
        in_specs=[pl.BlockSpec((tq, 512), rowmap), pl.BlockSpec((t, 256), lambda bi, qi: (bi, 0)),
                  pl.BlockSpec((SEL_BLOCK, 256), const), pl.BlockSpec((SEL_BLOCK, 256), const)],
        out_specs=[pl.BlockSpec((tq, 512), rowmap), pl.BlockSpec((1, 2 * n_s, tq), lambda bi, qi: (bi, 0, qi))],
        out_shape=[jax.ShapeDtypeStruct((n, 512), F32), jax.ShapeDtypeStruct((b, 2 * n_s, t), BF16)],
        scratch_shapes=[pltpu.VMEM((2 * n_s, 256), F32)],
        compiler_params=_cparams(("parallel", "arbitrary")),
        name="cmp_p",
    )(nsaq, ckv, lw["pwe"], lw["pwo"])


G_PAGES = 8
G_DEC = 16


def _dec_kernel(mode, n_pages, n_sp, pt_ref, *refs):
    refs = list(refs)
    nsa = mode == "sel"
    g_n = G_DEC
    q_ref, new_ref = refs[0], refs[1]
    pages = refs[2:2 + g_n]
    pos = 2 + g_n
    tri_ref = sm_ref = rtn_ref = None
    rts = None
    if mode == "sb":
        tri_ref = refs[pos]; pos += 1
    if mode == "fox":
        rtn_ref = refs[pos]; pos += 1
        rts = refs[pos:pos + g_n]; pos += g_n
    if mode == "sel":
        sm_ref = refs[pos]; pos += 1
    o_ref, qbd_ref, acc_ref, m_ref, l_ref = refs[pos:pos + 5]
    cf_ref = refs[pos + 5] if mode == "fox" else None
    dead_ref = refs[pos + 5] if mode == "sb" else None

    t_s = q_ref.shape[0]
    nh = H_NSA if nsa else H_SB
    rows = nh * t_s
    kw = 128 if nsa else 256
    step = pl.program_id(1)
    n_steps = pl.num_programs(1)
    lane_lo = _iota((1, LANES), 1) < HEAD_DIM
    t_of_row = _iota((rows, 1), 0) % t_s

    def head_lanes(n, width):
        ln = _iota((1, width), 1)
        return (ln >= HEAD_DIM * n) & (ln < HEAD_DIM * (n + 1))

    def block(kt32, vt32, mask, bias):
        ktb = kt32.astype(BF16)
        vtb = vt32.astype(BF16)
        s = _dot(qbd_ref[...], ktb)
        if mode == "sb":
            sp = _softplus(s)
            spm = sp if mask is None else jnp.where(mask, sp, 0.0)
            tri = tri_ref[...]
            later = m_ref[...]
            als = []
            for c in reversed(range(s.shape[1] // PAGE)):
                spc = spm[:, PAGE * c:PAGE * (c + 1)]
                hi, lo = _split2(spc)
                alc = _dot(hi, tri) + _dot(lo, tri)
                als.append(alc + later)
                later = later + alc[:, 0:1] + spc[:, 0:1]
            al = als[0] if len(als) == 1 else jnp.concatenate(als[::-1], axis=1)
            a = jnp.exp((s - sp) - al)
            if mask is not None:
                a = jnp.where(mask, a, 0.0)
            acc_ref[...] += _dot_nt(a.astype(BF16), vtb)
            m_ref[...] = later
        else:
            if bias is not None:
                s = s + bias
            if mask is not None:
                s = jnp.where(mask, s, NEG_INF)
            m_old = m_ref[...]
            m_new = jnp.maximum(m_old, jnp.max(s, axis=-1, keepdims=True))
            p = jnp.exp(s - m_new)
            alpha = jnp.exp(m_old - m_new)
            l_ref[...] = alpha * l_ref[...] + jnp.sum(p, axis=-1, keepdims=True)
            acc_ref[...] = alpha * acc_ref[...] + _dot_nt(p.astype(BF16), vtb)
            m_ref[...] = m_new

    def rows_of(vecs):
        return jnp.concatenate([jnp.broadcast_to(v, (t_s, 128)) for v in vecs], axis=0)

    @pl.when(step == 0)
    def _first():
        q = q_ref[...]
        if nsa:
            parts = []
            for g in range(HKV_NSA):
                hm = lane_lo if g == 0 else jnp.logical_not(lane_lo)
                for r in range(GQA_R):
                    parts.append(jnp.where(hm, q[:, 128 * r:128 * (r + 1)], jnp.zeros((), BF16)))
        else:
            parts = [jnp.where(head_lanes(n, 256), q, jnp.zeros((), BF16)) for n in range(nh)]
        qbd_ref[...] = jnp.concatenate(parts, axis=0)
        acc_ref[...] = jnp.zeros_like(acc_ref)
        l_ref[...] = jnp.zeros_like(l_ref)
        m_ref[...] = jnp.zeros_like(m_ref) if mode == "sb" else jnp.full_like(m_ref, NEG_INF)
        key = _iota((rows, PAGE), 1)
        mask = (key < t_of_row) if mode == "sb" else (key <= t_of_row)
        bias = None
        if mode == "fox":
            cf_ref[...] = jnp.zeros_like(cf_ref)
            bias = -rows_of([rtn_ref[0, :, 512 + 128 * n:512 + 128 * (n + 1)] for n in range(nh)])
        if mode == "sb":
            dead_ref[0] = 0
        block(new_ref[0, 0:kw, :], new_ref[0, kw:2 * kw, :], mask, bias)

    live = (dead_ref[0] == 0) if mode == "sb" else True

    @pl.when((step > 0) & live)
    def _pages():
        bias = None
        mask = None
        if mode == "fox":
            later = cf_ref[...]
            pieces = []
            for g in reversed(range(g_n)):
                rt = rts[g]
                pieces.append(rows_of([rt[0, :, 128 * n:128 * (n + 1)] for n in range(nh)]) + later)
                later = later + rows_of([rt[0, :, 1024 + 128 * n:1024 + 128 * (n + 1)] for n in range(nh)])
            bias = jnp.concatenate(pieces[::-1], axis=1)
            cf_ref[...] = later
        if mode == "sel":
            nk = g_n * PAGE
            first_pos = (n_pages - g_n * step) * PAGE
            blk = (first_pos + _iota((2 * n_sp, nk), 1)) // SEL_BLOCK
            e = (_iota((2 * n_sp, nk), 0) % n_sp) == blk
            grp = _iota((rows, 2 * n_sp), 0) // (GQA_R * t_s) == _iota((rows, 2 * n_sp), 1) // n_sp
            smr = jnp.concatenate([sm_ref[...]] * nh, axis=0)
            smr = jnp.where(grp, smr, jnp.zeros((), BF16))
            mask = _dot(smr, e.astype(BF16)) > 0.5
        kt = jnp.concatenate([p[0, 0:kw, :] for p in pages], axis=1)
        vt = jnp.concatenate([p[0, kw:2 * kw, :] for p in pages], axis=1)
        block(kt, vt, mask, bias)
        if mode == "sb":
            dead_ref[0] = (jnp.min(m_ref[...]) > SB_DEAD).astype(jnp.int32)

    @pl.when(step == n_steps - 1)
    def _fin():
        o = acc_ref[...] if mode == "sb" else acc_ref[...] / l_ref[...]
        if nsa:
            for r in range(GQA_R):
                o0 = o[t_s * r:t_s * (r + 1), :]
                o1 = o[t_s * (GQA_R + r):t_s * (GQA_R + r + 1), :]
                o_ref[:, 128 * r:128 * (r + 1)] = jnp.where(lane_lo, o0, o1)
        else:
            out = jnp.zeros((t_s, 256), F32)
            for n in range(nh):
                out = out + jnp.where(head_lanes(n, 256), o[t_s * n:t_s * (n + 1), :], 0.0)
            o_ref[...] = out


def _dec(mode, q, newkv_t, cache_t, layer, page_tab, extra=None, rt_new=None, rt_cache=None):
    nsa = mode == "sel"
    bs, n_pages = page_tab.shape
    t_s = q.shape[0] // bs
    wq = q.shape[1]
    wkv = cache_t.shape[1]
    g_n = G_DEC
    n_steps = 1 + n_pages // g_n
    n_sp = n_pages * PAGE // SEL_BLOCK
    nh = H_NSA if nsa else H_SB
    rows = nh * t_s

    def page_map(g, base):
        def f(b, s, pt):
            return (base + pt[b * n_pages + n_pages - g_n * jnp.maximum(s, 1) + g], 0, 0)
        return f

    base = layer * (cache_t.shape[0] // 2)
    in_specs = [pl.BlockSpec((t_s, wq), lambda b, s, pt: (b, 0)),
                pl.BlockSpec((1, wkv, PAGE), lambda b, s, pt: (b, 0, 0))]
    in_specs += [pl.BlockSpec((1, wkv, PAGE), page_map(g, base)) for g in range(g_n)]
    args = [q, newkv_t] + [cache_t] * g_n
    if mode == "sb":
        in_specs.append(pl.BlockSpec((PAGE, PAGE), lambda b, s, pt: (0, 0)))
        args.append(extra)
    if mode == "fox":
        in_specs.append(pl.BlockSpec((1, 1, 1536), lambda b, s, pt: (b, 0, 0)))
        args.append(rt_new)
        in_specs += [pl.BlockSpec((1, 1, 1536), page_map(g, 0)) for g in range(g_n)]
        args += [rt_cache] * g_n
    if mode == "sel":
        in_specs.append(pl.BlockSpec((t_s, 2 * n_sp), lambda b, s, pt: (b, 0)))
        args.append(extra)
    scratch = [pltpu.VMEM((rows, 128 if nsa else 256), BF16), pltpu.VMEM((rows, 128 if nsa else 256), F32),
               pltpu.VMEM((rows, 1), F32), pltpu.VMEM((rows, 1), F32)]
    if mode == "fox":
        scratch.append(pltpu.VMEM((rows, 128), F32))
    if mode == "sb":
        scratch.append(pltpu.SMEM((1,), jnp.int32))
    return pl.pallas_call(
        functools.partial(_dec_kernel, mode, n_pages, n_sp),
        grid_spec=pltpu.PrefetchScalarGridSpec(
            num_scalar_prefetch=1, grid=(bs, n_steps), in_specs=in_specs,
            out_specs=pl.BlockSpec((t_s, wq), lambda b, s, pt: (b, 0)), scratch_shapes=scratch),
        out_shape=jax.ShapeDtypeStruct((bs * t_s, wq), F32),
        compiler_params=_cparams(("parallel", "arbitrary")),
        name="dec_" + mode,
    )(page_tab.reshape(-1), *args)


def _cmp_s_kernel(n_pages, pt_ref, q_ref, *refs):
    g_n = G_PAGES
    pages = refs[0:g_n]
    wph_ref, wpl_ref, o_ref, sm_ref, kc_ref = refs[g_n:g_n + 5]
    t_s = q_ref.shape[0]
    n_sp = n_pages * PAGE // SEL_BLOCK
    n_c = 2 * n_sp
    per = g_n * PAGE // SEL_BLOCK
    step = pl.program_id(1)
    x = jnp.concatenate([p[0] for p in pages], axis=1)
    xh, xl = _split2(x)
    wh, wl = wph_ref[...], wpl_ref[...]
    pooled_t = (_dot(xh, wh) + _dot(xl, wh) + _dot(xh, wl)).T
    lane = _iota((1, 256), 1)
    pt = jnp.zeros((2 * per, 256), F32)
    for r in range(4):
        own = (lane >= HEAD_DIM * r) & (lane < HEAD_DIM * (r + 1))
        pt = pt + jnp.where(own, pooled_t[2 * per * r:2 * per * (r + 1), :], 0.0)
    off = pl.multiple_of(step * per, per)
    kc_ref[pl.ds(off, per), :] = pt[0:per, :]
    kc_ref[pl.ds(n_sp + off, per), :] = pt[per:2 * per, :]

    @pl.when(step == pl.num_programs(1) - 1)
    def _fin():
        kcb = kc_ref[:, 0:128].astype(BF16)
        vcb = kc_ref[:, 128:256].astype(BF16)
        lane_lo = _iota((1, LANES), 1) < HEAD_DIM
        q = q_ref[...]
        imps = []
        outs = []
        for g in range(HKV_NSA):
            hm = lane_lo if g == 0 else jnp.logical_not(lane_lo)
            qg = jnp.concatenate([jnp.where(hm, q[:, 128 * r:128 * (r + 1)], jnp.zeros((), BF16))
                                  for r in range(GQA_R)], axis=0)
            s = _dot_nt(qg, kcb)
            e = jnp.exp(s - jnp.max(s, axis=-1, keepdims=True))
            p = e / jnp.maximum(jnp.sum(e, axis=-1, keepdims=True), TINY)
            outs.append(_dot(p.astype(BF16), vcb))
            imp = jnp.zeros((t_s, n_sp), F32)
            for r in range(GQA_R):
                pr = p[t_s * r:t_s * (r + 1), :]
                imp = imp + (pr[:, 0:n_sp] + pr[:, n_sp:n_c])
            imps.append(imp)
        for r in range(GQA_R):
            o_ref[:, 128 * r:128 * (r + 1)] = jnp.where(
                lane_lo, outs[0][t_s * r:t_s * (r + 1), :], outs[1][t_s * r:t_s * (r + 1), :])
        imp = jnp.concatenate(imps, axis=0)
        jb = _iota((2 * t_s, n_sp), 1)
        sc = jnp.where((jb == 0) | (jb == n_sp - 1), FORCE_SCORE, imp)
        chosen = jnp.zeros(sc.shape, F32)
        for _ in range(min(SEL_TOPK - 1, n_sp)):
            idx = jnp.argmax(sc, axis=1).astype(jnp.int32)[:, None]
            hit = jb == idx
            chosen = jnp.where(hit, 1.0, chosen)
            sc = jnp.where(hit, -jnp.inf, sc)
        sm_ref[...] = jnp.concatenate([chosen[0:t_s], chosen[t_s:2 * t_s]], axis=1).astype(BF16)


def _cmp_s(nsaq, cache_t, layer, page_tab, lw):
    bs, n_pages = page_tab.shape
    t_s = nsaq.shape[0] // bs
    g_n = G_PAGES
    n_sp = n_pages * PAGE // SEL_BLOCK
    base = layer * (cache_t.shape[0] // 2)

    def page_map(g):
        return lambda b, s, pt: (base + pt[b * n_pages + g_n * s + g], 0, 0)

    const2 = lambda b, s, pt: (0, 0)
    return pl.pallas_call(
        functools.partial(_cmp_s_kernel, n_pages),
        grid_spec=pltpu.PrefetchScalarGridSpec(
            num_scalar_prefetch=1, grid=(bs, n_pages // g_n),
            in_specs=[pl.BlockSpec((t_s, 512), lambda b, s, pt: (b, 0))]
                     + [pl.BlockSpec((1, 256, PAGE), page_map(g)) for g in range(g_n)]
                     + [pl.BlockSpec((g_n * PAGE, 128), const2), pl.BlockSpec((g_n * PAGE, 128), const2)],
            out_specs=[pl.BlockSpec((t_s, 512), lambda b, s, pt: (b, 0)),
                       pl.BlockSpec((t_s, 2 * n_sp), lambda b, s, pt: (b, 0))],
            scratch_shapes=[pltpu.VMEM((2 * n_sp, 256), F32)]),
        out_shape=[jax.ShapeDtypeStruct((bs * t_s, 512), F32), jax.ShapeDtypeStruct((bs * t_s, 2 * n_sp), BF16)],
        compiler_params=_cparams(("parallel", "arbitrary")),
        name="cmp_s",
    )(page_tab.reshape(-1), nsaq, *([cache_t] * g_n), lw["wph"], lw["wpl"])


def _win_s_kernel(q_ref, st_ref, new_ref, o_ref):
    t_s = q_ref.shape[0]
    n_w = st_ref.shape[2]
    rows = H_NSA * t_s
    lane_lo = _iota((1, LANES), 1) < HEAD_DIM
    q = q_ref[...]
    parts = []
    for g in range(HKV_NSA):
        hm = lane_lo if g == 0 else jnp.logical_not(lane_lo)
        for r in range(GQA_R):
            parts.append(jnp.where(hm, q[:, 128 * r:128 * (r + 1)], jnp.zeros((), BF16)))
    qbd = jnp.concatenate(parts, axis=0)
    t_of_row = _iota((rows, 1), 0) % t_s
    jp = _iota((rows, n_w), 1)
    mask_p = (jp > t_of_row + (n_w - WINDOW)) & (jp <= t_of_row + n_w)
    jn = _iota((rows, PAGE), 1)
    mask_n = jn <= t_of_row
    s_p = jnp.where(mask_p, _dot(qbd, st_ref[0, 0:128, :].astype(BF16)), NEG_INF)
    s_n = jnp.where(mask_n, _dot(qbd, new_ref[0, 0:128, :].astype(BF16)), NEG_INF)
    mx = jnp.maximum(jnp.max(s_p, axis=-1, keepdims=True), jnp.max(s_n, axis=-1, keepdims=True))
    p_p = jnp.exp(s_p - mx)
    p_n = jnp.exp(s_n - mx)
    den = jnp.sum(p_p, axis=-1, keepdims=True) + jnp.sum(p_n, axis=-1, keepdims=True)
    o = (_dot_nt(p_p.astype(BF16), st_ref[0, 128:256, :].astype(BF16))
         + _dot_nt(p_n.astype(BF16), new_ref[0, 128:256, :].astype(BF16))) / den
    for r in range(GQA_R):
        o_ref[:, 128 * r:128 * (r + 1)] = jnp.where(
            lane_lo, o[t_s * r:t_s * (r + 1), :], o[t_s * (GQA_R + r):t_s * (GQA_R + r + 1), :])


def _win_s(nsaq, state_t, layer, newkv_t, bs):
    t_s = nsaq.shape[0] // bs
    n_w = state_t.shape[2]
    return pl.pallas_call(
        _win_s_kernel,
        grid=(bs,),
        in_specs=[pl.BlockSpec((t_s, 512), lambda b: (b, 0)),
                  pl.BlockSpec((1, 256, n_w), lambda b: (layer * bs + b, 0, 0)),
                  pl.BlockSpec((1, 256, PAGE), lambda b: (b, 0, 0))],
        out_specs=pl.BlockSpec((t_s, 512), lambda b: (b, 0)),
        out_shape=jax.ShapeDtypeStruct((bs * t_s, 512), F32),
        compiler_params=_cparams(("parallel",)),
        name="win_s",
    )(nsaq, state_t, newkv_t)


def _merge_kernel(x_ref, g_ref, osb_ref, ofox_ref, ocmp_ref, osel_ref, owin_ref, misc_ref,
                  wmg_ref, wsb_ref, wfox_ref, wnsa_ref, wo_ref, ex_ref, y_ref):
    x = x_ref[...]
    h = x * lax.rsqrt(jnp.mean(x * x, axis=-1, keepdims=True) + EPS) * g_ref[...]
    hb = h.astype(BF16)
    mhi, mlo = _split2(misc_ref[...])

    def gate(i):
        ex = ex_ref[:, 512 * i:512 * (i + 1)]
        return _dot(mhi, ex) + _dot(mlo, ex)

    o_nsa = gate(0) * ocmp_ref[...] + gate(1) * osel_ref[...] + gate(2) * owin_ref[...]
    d = x.shape[1]

    def mg(i):
        return jax.nn.sigmoid(_dot_nt(hb, wmg_ref[d * i:d * (i + 1), :]))

    mixed = (mg(0) * _dot(osb_ref[...].astype(BF16), wsb_ref[...])
             + mg(1) * _dot(ofox_ref[...].astype(BF16), wfox_ref[...])
             + mg(2) * _dot(o_nsa.astype(BF16), wnsa_ref[...]))
    y_ref[...] = x + _dot(mixed.astype(BF16), wo_ref[...])


def _merge(x2, o_sb, o_fox, o_cmp, o_sel, o_win, misc, lw, tm):
    n, d = x2.shape
    row = lambda i: (i, 0)
    const = lambda i: (0, 0)
    return pl.pallas_call(
        _merge_kernel,
        grid=(n // tm,),
        in_specs=[pl.BlockSpec((tm, d), row), pl.BlockSpec((1, d), const),
                  pl.BlockSpec((tm, 256), row), pl.BlockSpec((tm, 256), row),
                  pl.BlockSpec((tm, 512), row), pl.BlockSpec((tm, 512), row), pl.BlockSpec((tm, 512), row),
                  pl.BlockSpec((tm, 128), row),
                  pl.BlockSpec((N_MG, d), const), pl.BlockSpec((256, d), const), pl.BlockSpec((256, d), const),
                  pl.BlockSpec((512, d), const), pl.BlockSpec((d, d), const), pl.BlockSpec((128, 1536), const)],
        out_specs=pl.BlockSpec((tm, d), row),
        out_shape=jax.ShapeDtypeStruct((n, d), F32),
        compiler_params=_cparams(("parallel",)),
        name="merge",
    )(x2, lw["attn_norm"], o_sb, o_fox, o_cmp, o_sel, o_win, misc,
      lw["w_mg"], lw["w_br_sb"], lw["w_br_fox"], lw["w_br_nsa"], lw["w_o"], lw["ex"])


FF_CHUNK = 768


def _ffn_kernel(x_ref, g_ref, hist_ref, wup_ref, cw_ref, cb_ref, wdn_ref, y_ref, cs_ref, prev_ref):
    tm = x_ref.shape[0]
    d_ff = wdn_ref.shape[0]
    ti = pl.program_id(1)

    @pl.when(ti == 0)
    def _hist():
        prev_ref[...] = hist_ref[0]

    x = x_ref[...]
    h = x * lax.rsqrt(jnp.mean(x * x, axis=-1, keepdims=True) + EPS) * g_ref[...]
    hb = h.astype(BF16)
    rowi = _iota((tm, 1), 0)
    y = x
    for c in range(0, d_ff, FF_CHUNK):
        gate = _dot(hb, wup_ref[:, c:c + FF_CHUNK])
        up = _dot(hb, wup_ref[:, d_ff + c:d_ff + c + FF_CHUNK])
        p6 = prev_ref[6:7, c:c + FF_CHUNK]
        p7 = prev_ref[7:8, c:c + FF_CHUNK]
        g1 = jnp.where(rowi == 0, p7, pltpu.roll(gate, 1, 0))
        g2 = jnp.where(rowi == 0, p6, jnp.where(rowi == 1, p7, pltpu.roll(gate, 2, 0)))
        conv = (cb_ref[:, c:c + FF_CHUNK] + g2 * cw_ref[0:1, c:c + FF_CHUNK]
                + g1 * cw_ref[1:2, c:c + FF_CHUNK] + gate * cw_ref[2:3, c:c + FF_CHUNK])
        act = conv * jax.nn.sigmoid(conv) * up
        y = y + _dot(act.astype(BF16), wdn_ref[c:c + FF_CHUNK, :])
        tail = gate[tm - 8:tm, :]
        prev_ref[:, c:c + FF_CHUNK] = tail
        cs_ref[0, :, c:c + FF_CHUNK] = tail
    y_ref[...] = y


def _ffn(x2, hist, lw, nb, tm):
    n, d = x2.shape
    nt = n // nb // tm
    d_ff = lw["w_down"].shape[0]
    row = lambda b, i: (b * nt + i, 0)
    const = lambda b, i: (0, 0)
    hmap = (lambda b, i: (b, 0, 0)) if hist.shape[0] == nb else (lambda b, i: (0, 0, 0))
    once = dict(pipeline_mode=pl.Buffered(1))
    return pl.pallas_call(
        _ffn_kernel,
        grid=(nb, nt),
        in_specs=[pl.BlockSpec((tm, d), row), pl.BlockSpec((1, d), const), pl.BlockSpec((1, 8, d_ff), hmap),
                  pl.BlockSpec((d, 2 * d_ff), const, **once), pl.BlockSpec((CONV_W, d_ff), const),
                  pl.BlockSpec((1, d_ff), const), pl.BlockSpec((d_ff, d), const, **once)],
        out_specs=[pl.BlockSpec((tm, d), row), pl.BlockSpec((1, 8, d_ff), lambda b, i: (b, 0, 0))],
        out_shape=[jax.ShapeDtypeStruct((n, d), F32), jax.ShapeDtypeStruct((nb, 8, d_ff), F32)],
        scratch_shapes=[pltpu.VMEM((8, d_ff), F32)],
        compiler_params=_cparams(("parallel", "arbitrary")),
        name="ffn",
    )(x2, lw["ffn_norm"], hist, lw["w_up"], lw["conv_w"], lw["conv_b"], lw["w_down"])


def _consts():
    bd = (np.arange(256)[:, None] // HEAD_DIM == np.arange(256)[None, :] // HEAD_DIM)
    tri_rows = np.arange(256)[:, None] < np.arange(256)[None, :]
    tri128 = np.arange(PAGE)[:, None] > np.arange(PAGE)[None, :]
    hd, ps = np.arange(512) // 128, np.arange(512) % 128
    same = hd[:, None] == hd[None, :]
    c_r = same & (ps[:, None] > ps[None, :])
    c_w = same & (ps[:, None] <= ps[None, :])
    cmat = np.concatenate([c_r, c_w, same], axis=1)
    ex = np.zeros((128, 1536), np.float32)
    for br in range(3):
        for g in range(HKV_NSA):
            for r in range(GQA_R):
                m = 2 * r + g
                ex[H_FOX + br * 8 + g * GQA_R + r, 512 * br + 64 * m:512 * br + 64 * (m + 1)] = 1.0
    as_bf = lambda a: jnp.asarray(a.astype(np.float32), BF16)
    return dict(bd=as_bf(bd), tri_rows=as_bf(tri_rows), tri128=as_bf(tri128), cmat=as_bf(cmat), ex=as_bf(ex))


def _pool_matrices(pool):
    n_pos = G_PAGES * PAGE
    pos = np.arange(n_pos)
    blk = pos // CMP_BLOCK
    col = blk // 2 + (n_pos // SEL_BLOCK) * (blk % 2)
    w = jnp.zeros((n_pos, 128), F32)
    for kv in range(2):
        for g in range(HKV_NSA):
            r = 2 * kv + g
            place = jnp.asarray((32 * r + col[:, None] == np.arange(128)[None, :]).astype(np.float32))
            w = w + place * pool[kv, :, g][pos % CMP_BLOCK][:, None]
    hi = w.astype(BF16)
    lo = (w - hi.astype(F32)).astype(BF16)
    return hi, lo


def _layer_weights(l, cst, attn_norm, w_in, fox_forget_bias, fox_qk_gain, nsa_qk_gain, nsa_cmp_pool,
                   w_br_sb, w_br_fox, w_br_nsa, w_o, ffn_norm, w_up, conv_w, conv_b, w_down):
    w = w_in[l].T
    d = w.shape[1]
    o_foxf = 6 * 256
    o_nsaq = o_foxf + H_FOX
    o_kv = o_nsaq + H_NSA * HEAD_DIM
    o_gate = o_kv + 6 * 128
    o_mg = o_gate + 3 * H_NSA
    head_order = [(m % 2) * GQA_R + m // 2 for m in range(H_NSA)]
    nsaq_rows = jnp.concatenate([w[o_nsaq + 64 * hh:o_nsaq + 64 * (hh + 1)] for hh in head_order], axis=0)
    misc = jnp.concatenate([w[o_foxf:o_foxf + H_FOX], w[o_gate:o_gate + 3 * H_NSA],
                            jnp.zeros((128 - H_FOX - 3 * H_NSA, d), w.dtype)], axis=0)
    w_a = jnp.concatenate([w[0:o_foxf], nsaq_rows, w[o_kv:o_gate], misc], axis=0).astype(BF16)
    tile = lambda v, n: jnp.tile(v, n)[None, :]
    gains = jnp.concatenate([
        tile(fox_qk_gain[l, 0], 4), tile(fox_qk_gain[l, 1], 4), tile(nsa_qk_gain[l, 0], 4),
        tile(nsa_qk_gain[l, 1], 4), tile(nsa_qk_gain[l, 2], 4), tile(nsa_qk_gain[l, 3], 4),
        jnp.zeros((2, 256), F32)], axis=0)
    fbias = jnp.concatenate([fox_forget_bias[l], jnp.zeros((128 - H_FOX,), F32)])[None, :]
    pool = nsa_cmp_pool[l]
    lanes = lambda p: jnp.repeat(p, HEAD_DIM, axis=1)
    pw = jnp.concatenate([lanes(pool[0]), lanes(pool[1])], axis=1)
    zero = jnp.zeros_like(pw)
    wph, wpl = _pool_matrices(pool)
    nsa_rows = jnp.concatenate([w_br_nsa[l, 64 * hh:64 * (hh + 1)] for hh in head_order], axis=0)
    return dict(
        attn_norm=attn_norm[l][None, :], w_a=w_a, w_mg=w[o_mg:].astype(BF16), bd=cst["bd"], gains=gains,
        fbias=fbias, pwe=jnp.concatenate([pw, zero], axis=0), pwo=jnp.concatenate([zero, pw], axis=0),
        wph=wph, wpl=wpl,
        w_br_sb=w_br_sb[l].astype(BF16), w_br_fox=w_br_fox[l].astype(BF16), w_br_nsa=nsa_rows.astype(BF16),
        w_o=w_o[l].astype(BF16), ex=cst["ex"], ffn_norm=ffn_norm[l][None, :], w_up=w_up[l].astype(BF16),
        conv_w=conv_w[l], conv_b=conv_b[l][None, :], w_down=w_down[l].astype(BF16))


def _rope_tables(pos):
    half = ROPE_DIM // 2
    inv = ROPE_THETA ** (-jnp.arange(half, dtype=F32) * (2.0 / ROPE_DIM))
    ang = pos.astype(F32)[:, None] * inv[None, :]
    cos, sin = jnp.cos(ang), jnp.sin(ang)
    t = pos.shape[0]
    one = jnp.ones((t, HEAD_DIM - ROPE_DIM), F32)
    zero8 = jnp.zeros((t, half), F32)
    zero48 = jnp.zeros((t, HEAD_DIM - ROPE_DIM), F32)
    two = lambda a: jnp.concatenate([a, a], axis=1)
    return (two(jnp.concatenate([cos, cos, one], axis=1)),
            two(jnp.concatenate([-sin, zero8, zero48], axis=1)),
            two(jnp.concatenate([zero8, sin, zero48], axis=1)))


def _new_block_t(a, bs):
    t = a.shape[0] // bs
    at = jnp.swapaxes(a.reshape(bs, t, a.shape[1]), 1, 2)
    return jnp.pad(at, ((0, 0), (0, 0), (0, PAGE - t)))


def _feature_major(c):
    depth, n, npos = c.shape[0:3]
    ct = jnp.moveaxis(c, 2, -1)
    return ct.reshape(depth * n, -1, npos)


def _prompt_layer(x2, b, t, lw, cst, tabs):
    tm = 512 if t % 512 == 0 else 256
    (sbk, foxk, nsak, nsaq, cmpkv, misc,
     sbqt, foxqt, nsaqt, sbkvt, foxkvt, cmpt, selt, wint, lft) = _proj(x2, lw, *tabs, tm, nb=b)
    o_sb = _attn("sb", sbqt, sbk, sbkvt, b, t, tri=cst["tri_rows"])
    o_fox = _attn("fox", foxqt, foxk, foxkvt, b, t, extra=_fox_bias(misc, cst["tri_rows"]))
    o_cmp, selm = _cmp_p(nsaq, cmpkv, lw, b, t)
    o_sel = _attn("sel", nsaqt, nsak, selt, b, t, kcol=1, extra=selm)
    o_win = _attn("win", nsaqt, nsak, wint, b, t, kcol=2)
    x_mid = _merge(x2, o_sb, o_fox, o_cmp, o_sel, o_win, misc, lw, 256)
    d_ff = lw["w_down"].shape[0]
    y, cs = _ffn(x_mid, jnp.zeros((1, 8, d_ff), F32), lw, b, 256)
    n_w = min(WINDOW, t)
    state = (sbkvt, foxkvt, lft[:, 0:H_FOX], cmpt, selt, wint[:, :, t - n_w:], cs[:, 8 - (CONV_W - 1):])
    return y, state


def _sample_layer(x2, bs, l, lw, cst, tabs, page_tab, c_sb, c_fox, c_logf, c_cmp, c_sel, st_win, st_conv):
    n = x2.shape[0]
    t_s = n // bs
    sbq, sbkv, foxq, foxkv, nsaq, cmpkv, selkv, winkv, misc = _proj(x2, lw, *tabs, n)
    logf_new = misc[:, 0:H_FOX]
    lf_new_pages = _new_block_t(logf_new, bs).reshape(bs, PAGE * H_FOX)
    rt_new = _logf_sums(lf_new_pages, cst["cmat"]).reshape(bs, 1, 1536)
    n_pool = c_logf.shape[0] // 2
    rt_cache = _logf_sums(c_logf[l * n_pool:(l + 1) * n_pool].reshape(n_pool, PAGE * H_FOX), cst["cmat"])
    rt_cache = rt_cache.reshape(n_pool, 1, 1536)
    o_sb = _dec("sb", sbq, _new_block_t(sbkv, bs), c_sb, l, page_tab, cst["tri128"])
    o_fox = _dec("fox", foxq, _new_block_t(foxkv, bs), c_fox, l, page_tab, rt_new=rt_new, rt_cache=rt_cache)
    o_cmp, selm = _cmp_s(nsaq, c_cmp, l, page_tab, lw)
    o_sel = _dec("sel", nsaq, _new_block_t(selkv, bs), c_sel, l, page_tab, selm)
    o_win = _win_s(nsaq, st_win, l, _new_block_t(winkv, bs), bs)
    x_mid = _merge(x2, o_sb, o_fox, o_cmp, o_sel, o_win, misc, lw, n)
    hist = jnp.pad(st_conv, ((0, 0), (8 - (CONV_W - 1), 0), (0, 0)))
    y, cs = _ffn(x_mid, hist, lw, bs, t_s)
    state = (sbkv, foxkv, logf_new, cmpkv, selkv, winkv, cs[:, 8 - (CONV_W - 1):])
    return y, state


def kernel(x_prompt, x_sample, cache_sb_kv, cache_fox_kv, cache_fox_logf, cache_nsa_cmp_kv, cache_nsa_sel_kv, state_nsa_win_kv, state_ffn_conv, page_table, attn_norm, w_in, fox_forget_bias, fox_qk_gain, nsa_qk_gain, nsa_cmp_pool, w_br_sb, w_br_fox, w_br_nsa, w_o, ffn_norm, w_up, conv_w, conv_b, w_down):
    b, t, d = x_prompt.shape
    bs, t_s, _ = x_sample.shape
    depth = w_in.shape[0]
    n_pages = page_table.shape[1]
    past = n_pages * PAGE
    n_w = state_nsa_win_kv.shape[2]
    assert t % 256 == 0 and t_s == 8 and n_pages % G_DEC == 0 and past >= WINDOW and depth == 2
    cst = _consts()
    tabs_p = _rope_tables(jnp.arange(t, dtype=jnp.int32))
    tabs_s = tuple(jnp.tile(a, (bs, 1)) for a in _rope_tables(past + jnp.arange(t_s, dtype=jnp.int32)))
    c_sb, c_fox = _feature_major(cache_sb_kv), _feature_major(cache_fox_kv)
    c_cmp, c_sel = _feature_major(cache_nsa_cmp_kv), _feature_major(cache_nsa_sel_kv)
    c_logf = _feature_major(cache_fox_logf)
    st_win = _feature_major(state_nsa_win_kv)
    hp = x_prompt.reshape(b * t, d)
    hs = x_sample.reshape(bs * t_s, d)
    st_p, st_s = [], []
    for l in range(depth):
        lw = _layer_weights(l, cst, attn_norm, w_in, fox_forget_bias, fox_qk_gain, nsa_qk_gain, nsa_cmp_pool,
                            w_br_sb, w_br_fox, w_br_nsa, w_o, ffn_norm, w_up, conv_w, conv_b, w_down)
        hp, new_p = _prompt_layer(hp, b, t, lw, cst, tabs_p)
        hs, new_s = _sample_layer(hs, bs, l, lw, cst, tabs_s, page_table, c_sb, c_fox, c_logf, c_cmp, c_sel,
                                  st_win, state_ffn_conv[l])
        st_p.append(new_p)
        st_s.append(new_s)

    def stack_p(i, shape):
        a = jnp.stack([s[i] for s in st_p], axis=0)
        a = a.reshape((depth, b) + shape + (a.shape[-1],))
        return jnp.moveaxis(a, -1, 2)

    def stack_s(i, shape):
        return jnp.stack([s[i].reshape((bs, t_s) + shape) for s in st_s], axis=0)

    kv = lambda h: (2, h, HEAD_DIM)
    win_new = stack_s(5, kv(HKV_NSA))
    win_s = jnp.concatenate([state_nsa_win_kv[:, :, t_s:], win_new], axis=2)[:, :, -n_w:]
    return (hp.reshape(b, t, d), hs.reshape(bs, t_s, d),
            stack_p(0, kv(H_SB)), stack_s(0, kv(H_SB)),
            stack_p(1, kv(H_FOX)), stack_s(1, kv(H_FOX)),
            stack_p(2, (H_FOX,)), stack_s(2, (H_FOX,)),
            stack_p(3, kv(HKV_NSA)), stack_s(3, kv(HKV_NSA)),
            stack_p(4, kv(HKV_NSA)), stack_s(4, kv(HKV_NSA)),
            stack_p(5, kv(HKV_NSA)), win_s,
            jnp.stack([s[6] for s in st_p], axis=0), jnp.stack([s[6] for s in st_s], axis=0))
```

```python
import functools

import numpy as np
import jax
import jax.numpy as jnp
from jax import lax
from jax.experimental import pallas as pl
from jax.experimental.pallas import tpu as pltpu

F32 = jnp.float32
BF16 = jnp.bfloat16

HEAD_DIM = 64
H_SB = 4
H_FOX = 4
H_NSA = 8
HKV_NSA = 2
GQA_R = H_NSA // HKV_NSA
ROPE_DIM = HEAD_DIM // 4
ROPE_THETA = 500000.0
CMP_BLOCK = 32
SEL_BLOCK = 64
SEL_TOPK = 16
WINDOW = 512
CONV_W = 3
EPS = 1e-6
SCALE = HEAD_DIM ** -0.5
NEG_INF = -1e30
TINY = 1e-30
FORCE_SCORE = 1e6
PAGE = 128
LANES = 128
VMEM_LIMIT = 56 * 1024 * 1024
SB_DEAD = 110.0

C_SBQ, C_SBKV, C_FOXQ, C_FOXK, C_FOXV = 0, 256, 768, 1024, 1280
C_NSAQ, C_CMP, C_SEL, C_WIN, C_MISC, C_END = 1536, 2048, 2304, 2560, 2816, 2944
N_MG = 3 * 1024


def _dot(a, b):
    return jnp.dot(a, b, preferred_element_type=F32)


def _dot_nt(a, b):
    return lax.dot_general(a, b, (((1,), (1,)), ((), ())), preferred_element_type=F32)


def _split2(x):
    hi = x.astype(BF16)
    lo = (x - hi.astype(F32)).astype(BF16)
    return hi, lo


def _split3(x):
    hi = x.astype(BF16)
    r = x - hi.astype(F32)
    mid = r.astype(BF16)
    lo = (r - mid.astype(F32)).astype(BF16)
    return hi, mid, lo


def _softplus(z):
    return jnp.maximum(z, 0.0) + jnp.log(1.0 + jnp.exp(-jnp.abs(z)))


def _iota(shape, dim):
    return lax.broadcasted_iota(jnp.int32, shape, dim)


def _cparams(sem):
    return pltpu.CompilerParams(dimension_semantics=sem, vmem_limit_bytes=VMEM_LIMIT)


def _proj_kernel(feature_major, x_ref, g_ref, w_ref, bd_ref, gains_ref, cos_ref, sa_ref, sb_ref, fb_ref, *outs):
    x = x_ref[...]
    h = x * lax.rsqrt(jnp.mean(x * x, axis=-1, keepdims=True) + EPS) * g_ref[...]
    hb = h.astype(BF16)

    def seg(lo, n):
        return _dot_nt(hb, w_ref[lo:lo + n, :])

    def hrms(z, gain):
        w = z.shape[1]
        hi, lo = _split2(z * z)
        bd = bd_ref[0:w, 0:w]
        ss = _dot(hi, bd) + _dot(lo, bd)
        return z * lax.rsqrt(ss * (1.0 / HEAD_DIM) + EPS) * gain

    def rope(z):
        return (z * cos_ref[...] + pltpu.roll(z, LANES - ROPE_DIM // 2, 1) * sa_ref[...]
                + pltpu.roll(z, ROPE_DIM // 2, 1) * sb_ref[...])

    sb_q = seg(C_SBQ, 256) * SCALE
    sb_kv = seg(C_SBKV, 512)
    fox_q = hrms(seg(C_FOXQ, 256), gains_ref[0:1, :]) * SCALE
    fox_k = hrms(seg(C_FOXK, 256), gains_ref[1:2, :])
    fox_v = seg(C_FOXV, 256)
    nsa_q = []
    for c in range(2):
        zq = hrms(seg(C_NSAQ + 256 * c, 256), gains_ref[2:3, :])
        nsa_q += [rope(zq[:, 128 * s:128 * (s + 1)]) * SCALE for s in range(2)]
    nsa_k, nsa_v = [], []
    for i, col in enumerate((C_CMP, C_SEL, C_WIN)):
        nsa_k.append(rope(hrms(seg(col, 128), gains_ref[3 + i:4 + i, 0:128])))
        nsa_v.append(seg(col + 128, 128))
    zm = seg(C_MISC, 128)
    lane = _iota((1, LANES), 1)
    misc = jnp.where(lane < H_FOX, -_softplus(-(zm + fb_ref[...])), jax.nn.sigmoid(zm))

    if not feature_major:
        sbq_ref, sbkv_ref, foxq_ref, foxkv_ref, nsaq_ref, cmp_ref, sel_ref, win_ref, misc_ref = outs
        sbq_ref[...] = sb_q.astype(BF16)
        sbkv_ref[...] = sb_kv
        foxq_ref[...] = fox_q.astype(BF16)
        foxkv_ref[:, 0:256] = fox_k
        foxkv_ref[:, 256:512] = fox_v
        for s in range(4):
            nsaq_ref[:, 128 * s:128 * (s + 1)] = nsa_q[s].astype(BF16)
        for i, ref in enumerate((cmp_ref, sel_ref, win_ref)):
            ref[:, 0:128] = nsa_k[i]
            ref[:, 128:256] = nsa_v[i]
        misc_ref[...] = misc
        return

    (sbk_ref, foxk_ref, nsak_ref, nsaq_ref, cmpkv_ref, misc_ref,
     sbqt_ref, foxqt_ref, nsaqt_ref, sbkvt_ref, foxkvt_ref, cmpt_ref, selt_ref, wint_ref, lft_ref) = outs
    sbk_ref[...] = sb_kv[:, 0:256].astype(BF16)
    foxk_ref[...] = fox_k.astype(BF16)
    for i in range(3):
        nsak_ref[:, 128 * i:128 * (i + 1)] = nsa_k[i].astype(BF16)
    for s in range(4):
        nsaq_ref[:, 128 * s:128 * (s + 1)] = nsa_q[s].astype(BF16)
        nsaqt_ref[0, 128 * s:128 * (s + 1), :] = nsa_q[s].T.astype(BF16)
    cmpkv_ref[:, 0:128] = nsa_k[0]
    cmpkv_ref[:, 128:256] = nsa_v[0]
    misc_ref[...] = misc
    sbqt_ref[0] = sb_q.T.astype(BF16)
    foxqt_ref[0] = fox_q.T.astype(BF16)
    sbkvt_ref[0] = sb_kv.T
    foxkvt_ref[0, 0:256, :] = fox_k.T
    foxkvt_ref[0, 256:512, :] = fox_v.T
    for i, ref in enumerate((cmpt_ref, selt_ref, wint_ref)):
        ref[0, 0:128, :] = nsa_k[i].T
        ref[0, 128:256, :] = nsa_v[i].T
    lft_ref[0] = misc.T[0:8, :]


def _proj(x2, lw, cos_t, sa_t, sb_t, tm, nb=None):
    n, d = x2.shape
    nt = cos_t.shape[0] // tm
    row = lambda i: (i, 0)
    const = lambda i: (0, 0)
    tab = pl.BlockSpec((tm, LANES), lambda i: (i % nt, 0))
    if nb is None:
        outs = [(256, BF16), (512, F32), (256, BF16), (512, F32), (512, BF16),
                (256, F32), (256, F32), (256, F32), (128, F32)]
        out_specs = [pl.BlockSpec((tm, w), row) for w, _ in outs]
        out_shape = [jax.ShapeDtypeStruct((n, w), dt) for w, dt in outs]
    else:
        t = n // nb
        nat = [(256, BF16), (256, BF16), (384, BF16), (512, BF16), (256, F32), (128, F32)]
        fm = [(256, BF16), (256, BF16), (512, BF16), (512, F32), (512, F32), (256, F32), (256, F32), (256, F32),
              (8, F32)]
        fmap = lambda i: (i // nt, 0, i % nt)
        out_specs = ([pl.BlockSpec((tm, w), row) for w, _ in nat]
                     + [pl.BlockSpec((1, w, tm), fmap) for w, _ in fm])
        out_shape = ([jax.ShapeDtypeStruct((n, w), dt) for w, dt in nat]
                     + [jax.ShapeDtypeStruct((nb, w, t), dt) for w, dt in fm])
    return pl.pallas_call(
        functools.partial(_proj_kernel, nb is not None),
        grid=(n // tm,),
        in_specs=[pl.BlockSpec((tm, d), row), pl.BlockSpec((1, d), const),
                  pl.BlockSpec((C_END, d), const), pl.BlockSpec((256, 256), const),
                  pl.BlockSpec((8, 256), const), tab, tab, tab, pl.BlockSpec((1, LANES), const)],
        out_specs=out_specs,
        out_shape=out_shape,
        compiler_params=_cparams(("parallel",)),
        name="proj",
    )(x2, lw["attn_norm"], lw["w_a"], lw["bd"], lw["gains"], cos_t, sa_t, sb_t, lw["fbias"])


def _logf_kernel(x_ref, c_ref, o_ref):
    parts = _split3(x_ref[...])
    c = c_ref[...]
    o_ref[...] = _dot(parts[0], c) + _dot(parts[1], c) + _dot(parts[2], c)


def _logf_sums(lf_pages, cmat):
    p = lf_pages.shape[0]
    tp = 256 if p % 256 == 0 else p
    return pl.pallas_call(
        _logf_kernel,
        grid=(p // tp,),
        in_specs=[pl.BlockSpec((tp, 512), lambda i: (i, 0)), pl.BlockSpec((512, 1536), lambda i: (0, 0))],
        out_specs=pl.BlockSpec((tp, 1536), lambda i: (i, 0)),
        out_shape=jax.ShapeDtypeStruct((p, 1536), F32),
        compiler_params=_cparams(("parallel",)),
        name="logf",
    )(lf_pages, cmat)


def _fox_bias_kernel(lf_ref, tri_ref, aug_ref, tt_ref):
    tk = lf_ref.shape[0]
    lf = lf_ref[...]
    tri = tri_ref[...]
    p3 = _split3(lf)
    rb = _dot(tri, p3[0]) + _dot(tri, p3[1]) + _dot(tri, p3[2])
    hi = rb.astype(BF16).astype(F32)
    r1 = rb - hi
    mid = r1.astype(BF16).astype(F32)
    lo = (r1 - mid).astype(BF16).astype(F32)
    li = _iota((tk, LANES), 1)
    comb = jnp.where(li < 20, pltpu.roll(hi, 16, 1), jnp.where(li < 24, pltpu.roll(mid, 20, 1), pltpu.roll(lo, 24, 1)))
    comb = jnp.where((li >= 16) & (li < 28), comb, 0.0)
    aug_ref[...] = jnp.where(li < 3, 1.0, comb).astype(BF16)
    tot = rb[0:1, :] + lf[0:1, :]
    diag = _iota((8, LANES), 0) == _iota((8, LANES), 1)
    tt_ref[0] = jnp.broadcast_to(jnp.sum(jnp.where(diag, tot, 0.0), axis=1, keepdims=True), tt_ref.shape[1:])


def _fox_bias(misc, tri, tq, tk=256):
    n = misc.shape[0]
    return pl.pallas_call(
        _fox_bias_kernel,
        grid=(n // tk,),
        in_specs=[pl.BlockSpec((tk, 128), lambda i: (i, 0)), pl.BlockSpec((tk, tk), lambda i: (0, 0))],
        out_specs=[pl.BlockSpec((tk, 128), lambda i: (i, 0)), pl.BlockSpec((1, 8, tq), lambda i: (i, 0, 0))],
        out_shape=[jax.ShapeDtypeStruct((n, 128), BF16), jax.ShapeDtypeStruct((n // tk, 8, tq), F32)],
        compiler_params=_cparams(("parallel",)),
        name="fox_bias",
    )(misc, tri)


def _attn_kernel(mode, nsa, tq, tk, n_s, qi_ref, kb_ref, *refs):
    refs = list(refs)
    qt_ref, k_ref, vt_ref = refs[0:3]
    pos = 3
    tri_ref = lf_ref = tt_ref = sm_ref = None
    if mode == "sb":
        tri_ref = refs[pos]; pos += 1
    if mode == "fox":
        lf_ref, tt_ref = refs[pos], refs[pos + 1]; pos += 2
    if mode == "sel":
        sm_ref = refs[pos]; pos += 1
    o_ref, acc_ref, m_ref = refs[pos:pos + 3]
    cf_ref = refs[pos + 3] if mode == "fox" else None
    dead_ref = refs[pos + 3] if mode == "sb" else None
    nh = H_NSA if nsa else H_SB

    p_id = pl.program_id(1)
    qi = qi_ref[p_id]
    kb = kb_ref[p_id]
    ratio = tq // tk
    first = kb == (qi + 1) * ratio - 1
    w_blocks = WINDOW // tk
    last = (kb == jnp.maximum(qi * ratio - w_blocks, 0)) if mode == "win" else (kb == 0)

    @pl.when(first)
    def _init():
        acc_ref[...] = jnp.zeros_like(acc_ref)
        m_ref[...] = jnp.zeros_like(m_ref) if mode == "sb" else jnp.full_like(m_ref, NEG_INF)
        if mode == "fox":
            cf_ref[...] = jnp.zeros_like(cf_ref)
        if mode == "sb":
            dead_ref[0] = 0

    row_lo = _iota((LANES, 1), 0) < HEAD_DIM

    def body(edge):
        mask = None
        if edge:
            d = (qi * tq + _iota((tk, tq), 1)) - (kb * tk + _iota((tk, tq), 0))
            if mode == "sb":
                mask = d > 0
            elif mode == "win":
                mask = (d >= 0) & (d < WINDOW)
            else:
                mask = d >= 0
        cf = None
        if mode == "fox":
            cf = cf_ref[...]
            cf_ref[...] = cf + tt_ref[0]

        hms, vts, qtms, kss = [], [], [], []
        for n in range(nh):
            slab, sub = divmod(n, 2)
            ko = 0 if nsa else 128 * slab
            hm = row_lo if sub == 0 else jnp.logical_not(row_lo)
            qtms.append(jnp.where(hm, qt_ref[128 * slab:128 * (slab + 1), :], jnp.zeros((), BF16)))
            kss.append(k_ref[:, ko:ko + 128])
            vo = 0 if nsa else 128 * slab
            vts.append(vt_ref[0, 0, vo:vo + 128, :].astype(BF16))
            hms.append(hm)
        if mode == "sb":
            ss = [_dot(kss[n], qtms[n]) for n in range(nh)]
            tri = tri_ref[...]
            sps, spms, hls = [], [], []
            for n in range(nh):
                sp = _softplus(ss[n])
                spm = sp if mask is None else jnp.where(mask, sp, 0.0)
                sps.append(sp)
                spms.append(spm)
                hls.append(_split2(spm))
            als = [_dot(tri, hi) + _dot(tri, lo) for hi, lo in hls]
            avs = []
            for n in range(nh):
                carry = m_ref[n]
                a = jnp.exp((ss[n] - sps[n]) - (als[n] + carry[0:1, :]))
                if mask is not None:
                    a = jnp.where(mask, a, 0.0)
                avs.append(a.astype(BF16))
                m_ref[n] = carry + (als[n][0:1, :] + spms[n][0:1, :])
            for n in range(nh):
                acc_ref[n] += _dot(vts[n], avs[n])
            dead_ref[0] = (jnp.min(m_ref[...]) > SB_DEAD).astype(jnp.int32)
        else:
            li = _iota((tk, LANES), 1)
            ones3 = li < 3
            ri16 = _iota((16, 1), 0)
            zeros16 = jnp.zeros((16, tq), BF16)
            if mode == "sel":
                smb = ((sm_ref[0].astype(F32) - 1.0) * 1e30).astype(BF16)
                blk = 16 + kb * (tk // SEL_BLOCK) + _iota((tk, LANES), 0) // SEL_BLOCK
                augl = jnp.where(ones3 | (li == blk), 1.0, 0.0).astype(BF16)
                pad = jnp.zeros((LANES - 16 - n_s, tq), BF16)
                bases = [jnp.concatenate([smb[g * n_s:(g + 1) * n_s], pad], axis=0) for g in range(2)]
                base_of = lambda n: bases[n % 2]
            elif mode == "fox":
                augl = lf_ref[...]
                pad = jnp.zeros((LANES - 32, tq), BF16)
                pick = [jnp.broadcast_to(jnp.where(((ri16 % 4) == n) & (ri16 < 12), 1.0, 0.0), (16, tq))
                        .astype(BF16) for n in range(nh)]
                base_of = lambda n: jnp.concatenate([pick[n], pad], axis=0)
            else:
                augl = jnp.where(ones3, 1.0, 0.0).astype(BF16)
                base_win = jnp.zeros((LANES - 16, tq), BF16)
                base_of = lambda n: base_win
            lhs = [jnp.concatenate([kss[n], augl], axis=1) for n in range(nh)]
            m_news, alphas = [], []
            for n in range(nh):
                s = _dot(lhs[n], jnp.concatenate([qtms[n], zeros16, base_of(n)], axis=0))
                if mask is not None:
                    s = jnp.where(mask, s, NEG_INF)
                m_old = m_ref[n]
                m_blk = jnp.max(s, axis=0, keepdims=True)
                if mode == "fox":
                    m_blk = m_blk + cf[n:n + 1, :]
                m_new = jnp.maximum(m_old, m_blk)
                m_news.append(m_new)
                alphas.append(jnp.exp(m_old - m_new))
                m_ref[n] = m_new
            ps = []
            for n in range(nh):
                shift = -m_news[n][0:1, :]
                if mode == "fox":
                    shift = shift + cf[n:n + 1, :]
                t3 = _split3(shift)
                terms = jnp.where(ri16 == 0, t3[0].astype(F32), jnp.where(
                    ri16 == 1, t3[1].astype(F32), jnp.where(ri16 == 2, t3[2].astype(F32), 0.0))).astype(BF16)
                s = _dot(lhs[n], jnp.concatenate([qtms[n], terms, base_of(n)], axis=0))
                if mask is not None:
                    s = jnp.where(mask, s, NEG_INF)
                ps.append(jnp.exp(s).astype(BF16))
            for n in range(nh):
                vaug = jnp.where(hms[n], vts[n], jnp.ones((), BF16))
                acc_ref[n] = acc_ref[n] * alphas[n][0:1, :] + _dot(vaug, ps[n])

    d_min = qi * tq - kb * tk - (tk - 1)
    d_max = qi * tq + (tq - 1) - kb * tk
    if mode == "win":
        edge = (d_min < 0) | (d_max >= WINDOW)
    elif mode == "sb":
        edge = d_min <= 0
    else:
        edge = d_min < 0
    live = (dead_ref[0] == 0) if mode == "sb" else True
    @pl.when(edge & live)
    def _edge():
        body(True)

    @pl.when(jnp.logical_not(edge) & live)
    def _inner():
        body(False)

    @pl.when(last)
    def _fin():
        for slab in range(nh // 2):
            a0, a1 = acc_ref[2 * slab], acc_ref[2 * slab + 1]
            if mode == "sb":
                o0, o1 = a0[0:HEAD_DIM, :], a1[HEAD_DIM:LANES, :]
            else:
                o0 = a0[0:HEAD_DIM, :] / a0[HEAD_DIM:HEAD_DIM + 1, :]
                o1 = a1[HEAD_DIM:LANES, :] / a1[0:1, :]
            o_ref[:, 128 * slab:128 * (slab + 1)] = jnp.concatenate([o0, o1], axis=0).T


def _pairs(nq, mode, tq, tk):
    ratio = tq // tk
    qi, kb = [], []
    for i in range(nq):
        lo = max(i * ratio - WINDOW // tk, 0) if mode == "win" else 0
        for k in range((i + 1) * ratio - 1, lo - 1, -1):
            qi.append(i)
            kb.append(k)
    return jnp.asarray(qi, jnp.int32), jnp.asarray(kb, jnp.int32)


def _attn(mode, qt, k, kvt, b, t, kcol=0, extra=None, tri=None, tq=512, tk=256):
    nsa = mode in ("sel", "win")
    wq = qt.shape[1]
    wv = kvt.shape[1] // 2
    wk = 128 if nsa else 256
    nq = t // tq
    n_s = t // SEL_BLOCK
    qi_tab, kb_tab = _pairs(nq, mode, tq, tk)
    in_specs = [pl.BlockSpec((1, wq, tq), lambda bi, p, qi, kb: (bi, 0, qi[p])),
                pl.BlockSpec((tk, wk), lambda bi, p, qi, kb: (bi * (t // tk) + kb[p], kcol)),
                pl.BlockSpec((1, 1, wv, tk), lambda bi, p, qi, kb: (bi, 1, 0, kb[p]))]
    args = [qt, k, kvt.reshape(b, 2, wv, t)]
    if mode == "sb":
        in_specs.append(pl.BlockSpec((tk, tk), lambda bi, p, qi, kb: (0, 0)))
        args.append(tri)
    if mode == "fox":
        in_specs.append(pl.BlockSpec((tk, 128), lambda bi, p, qi, kb: (bi * (t // tk) + kb[p], 0)))
        in_specs.append(pl.BlockSpec((1, 8, tq), lambda bi, p, qi, kb: (bi * (t // tk) + kb[p], 0, 0)))
        args += list(extra)
    if mode == "sel":
        in_specs.append(pl.BlockSpec((1, 2 * n_s, tq), lambda bi, p, qi, kb: (bi, 0, qi[p])))
        args.append(extra)
    nh = H_NSA if nsa else H_SB
    scratch = [pltpu.VMEM((nh, 128, tq), F32), pltpu.VMEM((nh, 8, tq), F32)]
    if mode == "fox":
        scratch.append(pltpu.VMEM((8, tq), F32))
    if mode == "sb":
        scratch.append(pltpu.SMEM((1,), jnp.int32))

    def kernel(qi_ref, kb_ref, qt_ref, *rest):
        _attn_kernel(mode, nsa, tq, tk, n_s, qi_ref, kb_ref, qt_ref.at[0], *rest)

    return pl.pallas_call(
        kernel,
        grid_spec=pltpu.PrefetchScalarGridSpec(
            num_scalar_prefetch=2, grid=(b, int(qi_tab.shape[0])),
            in_specs=in_specs, out_specs=pl.BlockSpec((tq, wq), lambda bi, p, qi, kb: (bi * nq + qi[p], 0)),
            scratch_shapes=scratch),
        out_shape=jax.ShapeDtypeStruct((b * t, wq), F32),
        compiler_params=_cparams(("parallel", "arbitrary")),
        name="attn_" + mode,
    )(qi_tab, kb_tab, *args)


def _pool_blocks(x, pwe_ref, pwo_ref):
    n = x.shape[0] // SEL_BLOCK
    x3 = x.reshape(n, SEL_BLOCK, 256)
    return (jnp.sum(x3 * pwe_ref[...][None], axis=1), jnp.sum(x3 * pwo_ref[...][None], axis=1))


def _rank_select(sc, n_blk, k_sel):
    j = _iota((n_blk, 1), 0)
    rank = jnp.zeros(sc.shape, F32)
    for i in range(n_blk):
        ri = sc[i:i + 1, :]
        beats = (ri > sc) | ((ri == sc) & (j > i))
        rank = rank + beats.astype(F32)
    return (rank < k_sel).astype(F32)


def _cmp_p_kernel(tq, t, q_ref, ckv_ref, pwe_ref, pwo_ref, o_ref, sm_ref, kc_ref):
    n_s = t // SEL_BLOCK
    n_c = 2 * n_s
    qi = pl.program_id(1)

    @pl.when(qi == 0)
    def _pool():
        ev, od = _pool_blocks(ckv_ref[...], pwe_ref, pwo_ref)
        kc_ref[0:n_s, :] = ev
        kc_ref[n_s:n_c, :] = od

    kcb = kc_ref[:, 0:128].astype(BF16)
    vcb = kc_ref[:, 128:256].astype(BF16)

    def c_end(i):
        blk = jnp.where(i < n_s, 2 * i, 2 * (i - n_s) + 1)
        return (blk + 1) * CMP_BLOCK - 1

    cmask = c_end(_iota((1, n_c), 1)) <= qi * tq + _iota((tq, 1), 0)
    t_row = qi * tq + _iota((1, tq), 1)
    cmask_t = c_end(_iota((n_c, 1), 0)) <= t_row
    lane_lo = _iota((1, LANES), 1) < HEAD_DIM
    imp = [jnp.zeros((n_s, tq), F32), jnp.zeros((n_s, tq), F32)]
    outs = {}
    for n in range(H_NSA):
        r, g = divmod(n, 2)
        hm = lane_lo if g == 0 else jnp.logical_not(lane_lo)
        qh = jnp.where(hm, q_ref[:, 128 * r:128 * (r + 1)], jnp.zeros((), BF16))
        s = jnp.where(cmask, _dot_nt(qh, kcb), NEG_INF)
        e = jnp.where(cmask, jnp.exp(s - jnp.max(s, axis=-1, keepdims=True)), 0.0)
        p = e / jnp.maximum(jnp.sum(e, axis=-1, keepdims=True), TINY)
        outs[(r, g)] = _dot(p.astype(BF16), vcb)
        st = jnp.where(cmask_t, _dot_nt(kcb, qh), NEG_INF)
        et = jnp.where(cmask_t, jnp.exp(st - jnp.max(st, axis=0, keepdims=True)), 0.0)
        pt = et / jnp.maximum(jnp.sum(et, axis=0, keepdims=True), TINY)
        imp[g] = imp[g] + (pt[0:n_s, :] + pt[n_s:n_c, :])
    for r in range(GQA_R):
        o_ref[:, 128 * r:128 * (r + 1)] = jnp.where(lane_lo, outs[(r, 0)], outs[(r, 1)])

    j = _iota((n_s, 1), 0)
    cur = t_row // SEL_BLOCK
    forced = (j == 0) | (j == cur) | (j == cur - 1)
    valid = j * SEL_BLOCK <= t_row
    sel = []
    for g in range(2):
        sc = jnp.where(forced, FORCE_SCORE, jnp.where(valid, imp[g], -1.0))
        sel.append(_rank_select(sc, n_s, min(SEL_TOPK, n_s)))
    sm_ref[0] = jnp.concatenate(sel, axis=0).astype(BF16)


def _cmp_p(nsaq, ckv, lw, b, t, tq=256):
    n = nsaq.shape[0]
    nq = t // tq
    n_s = t // SEL_BLOCK
    rowmap = lambda bi, qi: (bi * nq + qi, 0)
    const = lambda bi, qi: (0, 0)
    return pl.pallas_call(
        functools.partial(_cmp_p_kernel, tq, t),
        grid=(b, nq),
        in_specs=[pl.BlockSpec((tq, 512), rowmap), pl.BlockSpec((t, 256), lambda bi, qi: (bi, 0)),
                  pl.BlockSpec((SEL_BLOCK, 256), const), pl.BlockSpec((SEL_BLOCK, 256), const)],
        out_specs=[pl.BlockSpec((tq, 512), rowmap), pl.BlockSpec((1, 2 * n_s, tq), lambda bi, qi: (bi, 0, qi))],
        out_shape=[jax.ShapeDtypeStruct((n, 512), F32), jax.ShapeDtypeStruct((b, 2 * n_s, t), BF16)],
        scratch_shapes=[pltpu.VMEM((2 * n_s, 256), F32)],
        compiler_params=_cparams(("parallel", "arbitrary")),
        name="cmp_p",
    )(nsaq, ckv, lw["pwe"], lw["pwo"])


G_PAGES = 8
G_DEC = 16


def _dec_kernel(mode, n_pages, n_sp, pt_ref, *refs):
    refs = list(refs)
    nsa = mode == "sel"
    g_n = G_DEC
    q_ref, new_ref = refs[0], refs[1]
    pages = refs[2:2 + g_n]
    pos = 2 + g_n
    tri_ref = sm_ref = rtn_ref = None
    rts = None
    if mode == "sb":
        tri_ref = refs[pos]; pos += 1
    if mode == "fox":
        rtn_ref = refs[pos]; pos += 1
        rts = refs[pos:pos + g_n]; pos += g_n
    if mode == "sel":
        sm_ref = refs[pos]; pos += 1
    o_ref, qbd_ref, acc_ref, m_ref, l_ref = refs[pos:pos + 5]
    cf_ref = refs[pos + 5] if mode == "fox" else None
    dead_ref = refs[pos + 5] if mode == "sb" else None

    t_s = q_ref.shape[0]
    nh = H_NSA if nsa else H_SB
    rows = nh * t_s
    kw = 128 if nsa else 256
    step = pl.program_id(1)
    n_steps = pl.num_programs(1)
    lane_lo = _iota((1, LANES), 1) < HEAD_DIM
    t_of_row = _iota((rows, 1), 0) % t_s

    def head_lanes(n, width):
        ln = _iota((1, width), 1)
        return (ln >= HEAD_DIM * n) & (ln < HEAD_DIM * (n + 1))

    def block(kt32, vt32, mask, bias):
        ktb = kt32.astype(BF16)
        vtb = vt32.astype(BF16)
        s = _dot(qbd_ref[...], ktb)
        if mode == "sb":
            sp = _softplus(s)
            spm = sp if mask is None else jnp.where(mask, sp, 0.0)
            tri = tri_ref[...]
            later = m_ref[...]
            als = []
            for c in reversed(range(s.shape[1] // PAGE)):
                spc = spm[:, PAGE * c:PAGE * (c + 1)]
                hi, lo = _split2(spc)
                alc = _dot(hi, tri) + _dot(lo, tri)
                als.append(alc + later)
                later = later + alc[:, 0:1] + spc[:, 0:1]
            al = als[0] if len(als) == 1 else jnp.concatenate(als[::-1], axis=1)
            a = jnp.exp((s - sp) - al)
            if mask is not None:
                a = jnp.where(mask, a, 0.0)
            acc_ref[...] += _dot_nt(a.astype(BF16), vtb)
            m_ref[...] = later
        else:
            if bias is not None:
                s = s + bias
            if mask is not None:
                s = jnp.where(mask, s, NEG_INF)
            m_old = m_ref[...]
            m_new = jnp.maximum(m_old, jnp.max(s, axis=-1, keepdims=True))
            p = jnp.exp(s - m_new)
            alpha = jnp.exp(m_old - m_new)
            l_ref[...] = alpha * l_ref[...] + jnp.sum(p, axis=-1, keepdims=True)
            acc_ref[...] = alpha * acc_ref[...] + _dot_nt(p.astype(BF16), vtb)
            m_ref[...] = m_new

    def rows_of(vecs):
        return jnp.concatenate([jnp.broadcast_to(v, (t_s, 128)) for v in vecs], axis=0)

    @pl.when(step == 0)
    def _first():
        q = q_ref[...]
        if nsa:
            parts = []
            for g in range(HKV_NSA):
                hm = lane_lo if g == 0 else jnp.logical_not(lane_lo)
                for r in range(GQA_R):
                    parts.append(jnp.where(hm, q[:, 128 * r:128 * (r + 1)], jnp.zeros((), BF16)))
        else:
            parts = [jnp.where(head_lanes(n, 256), q, jnp.zeros((), BF16)) for n in range(nh)]
        qbd_ref[...] = jnp.concatenate(parts, axis=0)
        acc_ref[...] = jnp.zeros_like(acc_ref)
        l_ref[...] = jnp.zeros_like(l_ref)
        m_ref[...] = jnp.zeros_like(m_ref) if mode == "sb" else jnp.full_like(m_ref, NEG_INF)
        key = _iota((rows, PAGE), 1)
        mask = (key < t_of_row) if mode == "sb" else (key <= t_of_row)
        bias = None
        if mode == "fox":
            cf_ref[...] = jnp.zeros_like(cf_ref)
            bias = -rows_of([rtn_ref[0, :, 512 + 128 * n:512 + 128 * (n + 1)] for n in range(nh)])
        if mode == "sb":
            dead_ref[0] = 0
        block(new_ref[0, 0:kw, :], new_ref[0, kw:2 * kw, :], mask, bias)

    live = (dead_ref[0] == 0) if mode == "sb" else True

    @pl.when((step > 0) & live)
    def _pages():
        bias = None
        mask = None
        if mode == "fox":
            later = cf_ref[...]
            pieces = []
            for g in reversed(range(g_n)):
                rt = rts[g]
                pieces.append(rows_of([rt[0, :, 128 * n:128 * (n + 1)] for n in range(nh)]) + later)
                later = later + rows_of([rt[0, :, 1024 + 128 * n:1024 + 128 * (n + 1)] for n in range(nh)])
            bias = jnp.concatenate(pieces[::-1], axis=1)
            cf_ref[...] = later
        if mode == "sel":
            nk = g_n * PAGE
            first_pos = (n_pages - g_n * step) * PAGE
            blk = (first_pos + _iota((2 * n_sp, nk), 1)) // SEL_BLOCK
            e = (_iota((2 * n_sp, nk), 0) % n_sp) == blk
            grp = _iota((rows, 2 * n_sp), 0) // (GQA_R * t_s) == _iota((rows, 2 * n_sp), 1) // n_sp
            smr = jnp.concatenate([sm_ref[...]] * nh, axis=0)
            smr = jnp.where(grp, smr, jnp.zeros((), BF16))
            mask = _dot(smr, e.astype(BF16)) > 0.5
        kt = jnp.concatenate([p[0, 0:kw, :] for p in pages], axis=1)
        vt = jnp.concatenate([p[0, kw:2 * kw, :] for p in pages], axis=1)
        block(kt, vt, mask, bias)
        if mode == "sb":
            dead_ref[0] = (jnp.min(m_ref[...]) > SB_DEAD).astype(jnp.int32)

    @pl.when(step == n_steps - 1)
    def _fin():
        o = acc_ref[...] if mode == "sb" else acc_ref[...] / l_ref[...]
        if nsa:
            for r in range(GQA_R):
                o0 = o[t_s * r:t_s * (r + 1), :]
                o1 = o[t_s * (GQA_R + r):t_s * (GQA_R + r + 1), :]
                o_ref[:, 128 * r:128 * (r + 1)] = jnp.where(lane_lo, o0, o1)
        else:
            out = jnp.zeros((t_s, 256), F32)
            for n in range(nh):
                out = out + jnp.where(head_lanes(n, 256), o[t_s * n:t_s * (n + 1), :], 0.0)
            o_ref[...] = out


def _dec(mode, q, newkv_t, cache_t, layer, page_tab, extra=None, rt_new=None, rt_cache=None):
    nsa = mode == "sel"
    bs, n_pages = page_tab.shape
    t_s = q.shape[0] // bs
    wq = q.shape[1]
    wkv = cache_t.shape[1]
    g_n = G_DEC
    n_steps = 1 + n_pages // g_n
    n_sp = n_pages * PAGE // SEL_BLOCK
    nh = H_NSA if nsa else H_SB
    rows = nh * t_s

    def page_map(g, base):
        def f(b, s, pt):
            return (base + pt[b * n_pages + n_pages - g_n * jnp.maximum(s, 1) + g], 0, 0)
        return f

    base = layer * (cache_t.shape[0] // 2)
    in_specs = [pl.BlockSpec((t_s, wq), lambda b, s, pt: (b, 0)),
                pl.BlockSpec((1, wkv, PAGE), lambda b, s, pt: (b, 0, 0))]
    in_specs += [pl.BlockSpec((1, wkv, PAGE), page_map(g, base)) for g in range(g_n)]
    args = [q, newkv_t] + [cache_t] * g_n
    if mode == "sb":
        in_specs.append(pl.BlockSpec((PAGE, PAGE), lambda b, s, pt: (0, 0)))
        args.append(extra)
    if mode == "fox":
        in_specs.append(pl.BlockSpec((1, 1, 1536), lambda b, s, pt: (b, 0, 0)))
        args.append(rt_new)
        in_specs += [pl.BlockSpec((1, 1, 1536), page_map(g, 0)) for g in range(g_n)]
        args += [rt_cache] * g_n
    if mode == "sel":
        in_specs.append(pl.BlockSpec((t_s, 2 * n_sp), lambda b, s, pt: (b, 0)))
        args.append(extra)
    scratch = [pltpu.VMEM((rows, 128 if nsa else 256), BF16), pltpu.VMEM((rows, 128 if nsa else 256), F32),
               pltpu.VMEM((rows, 1), F32), pltpu.VMEM((rows, 1), F32)]
    if mode == "fox":
        scratch.append(pltpu.VMEM((rows, 128), F32))
    if mode == "sb":
        scratch.append(pltpu.SMEM((1,), jnp.int32))
    return pl.pallas_call(
        functools.partial(_dec_kernel, mode, n_pages, n_sp),
        grid_spec=pltpu.PrefetchScalarGridSpec(
            num_scalar_prefetch=1, grid=(bs, n_steps), in_specs=in_specs,
            out_specs=pl.BlockSpec((t_s, wq), lambda b, s, pt: (b, 0)), scratch_shapes=scratch),
        out_shape=jax.ShapeDtypeStruct((bs * t_s, wq), F32),
        compiler_params=_cparams(("parallel", "arbitrary")),
        name="dec_" + mode,
    )(page_tab.reshape(-1), *args)


def _cmp_s_kernel(n_pages, pt_ref, q_ref, *refs):
    g_n = G_PAGES
    pages = refs[0:g_n]
    wph_ref, wpl_ref, o_ref, sm_ref, kc_ref = refs[g_n:g_n + 5]
    t_s = q_ref.shape[0]
    n_sp = n_pages * PAGE // SEL_BLOCK
    n_c = 2 * n_sp
    per = g_n * PAGE // SEL_BLOCK
    step = pl.program_id(1)
    x = jnp.concatenate([p[0] for p in pages], axis=1)
    xh, xl = _split2(x)
    wh, wl = wph_ref[...], wpl_ref[...]
    pooled_t = (_dot(xh, wh) + _dot(xl, wh) + _dot(xh, wl)).T
    lane = _iota((1, 256), 1)
    pt = jnp.zeros((2 * per, 256), F32)
    for r in range(4):
        own = (lane >= HEAD_DIM * r) & (lane < HEAD_DIM * (r + 1))
        pt = pt + jnp.where(own, pooled_t[2 * per * r:2 * per * (r + 1), :], 0.0)
    off = pl.multiple_of(step * per, per)
    kc_ref[pl.ds(off, per), :] = pt[0:per, :]
    kc_ref[pl.ds(n_sp + off, per), :] = pt[per:2 * per, :]

    @pl.when(step == pl.num_programs(1) - 1)
    def _fin():
        kcb = kc_ref[:, 0:128].astype(BF16)
        vcb = kc_ref[:, 128:256].astype(BF16)
        lane_lo = _iota((1, LANES), 1) < HEAD_DIM
        q = q_ref[...]
        imps = []
        outs = []
        for g in range(HKV_NSA):
            hm = lane_lo if g == 0 else jnp.logical_not(lane_lo)
            qg = jnp.concatenate([jnp.where(hm, q[:, 128 * r:128 * (r + 1)], jnp.zeros((), BF16))
                                  for r in range(GQA_R)], axis=0)
            s = _dot_nt(qg, kcb)
            e = jnp.exp(s - jnp.max(s, axis=-1, keepdims=True))
            p = e / jnp.maximum(jnp.sum(e, axis=-1, keepdims=True), TINY)
            outs.append(_dot(p.astype(BF16), vcb))
            imp = jnp.zeros((t_s, n_sp), F32)
            for r in range(GQA_R):
                pr = p[t_s * r:t_s * (r + 1), :]
                imp = imp + (pr[:, 0:n_sp] + pr[:, n_sp:n_c])
            imps.append(imp)
        for r in range(GQA_R):
            o_ref[:, 128 * r:128 * (r + 1)] = jnp.where(
                lane_lo, outs[0][t_s * r:t_s * (r + 1), :], outs[1][t_s * r:t_s * (r + 1), :])
        imp = jnp.concatenate(imps, axis=0)
        jb = _iota((2 * t_s, n_sp), 1)
        sc = jnp.where((jb == 0) | (jb == n_sp - 1), FORCE_SCORE, imp)
        chosen = jnp.zeros(sc.shape, F32)
        for _ in range(min(SEL_TOPK - 1, n_sp)):
            idx = jnp.argmax(sc, axis=1).astype(jnp.int32)[:, None]
            hit = jb == idx
            chosen = jnp.where(hit, 1.0, chosen)
            sc = jnp.where(hit, -jnp.inf, sc)
        sm_ref[...] = jnp.concatenate([chosen[0:t_s], chosen[t_s:2 * t_s]], axis=1).astype(BF16)


def _cmp_s(nsaq, cache_t, layer, page_tab, lw):
    bs, n_pages = page_tab.shape
    t_s = nsaq.shape[0] // bs
    g_n = G_PAGES
    n_sp = n_pages * PAGE // SEL_BLOCK
    base = layer * (cache_t.shape[0] // 2)

    def page_map(g):
        return lambda b, s, pt: (base + pt[b * n_pages + g_n * s + g], 0, 0)

    const2 = lambda b, s, pt: (0, 0)
    return pl.pallas_call(
        functools.partial(_cmp_s_kernel, n_pages),
        grid_spec=pltpu.PrefetchScalarGridSpec(
            num_scalar_prefetch=1, grid=(bs, n_pages // g_n),
            in_specs=[pl.BlockSpec((t_s, 512), lambda b, s, pt: (b, 0))]
                     + [pl.BlockSpec((1, 256, PAGE), page_map(g)) for g in range(g_n)]
                     + [pl.BlockSpec((g_n * PAGE, 128), const2), pl.BlockSpec((g_n * PAGE, 128), const2)],
            out_specs=[pl.BlockSpec((t_s, 512), lambda b, s, pt: (b, 0)),
                       pl.BlockSpec((t_s, 2 * n_sp), lambda b, s, pt: (b, 0))],
            scratch_shapes=[pltpu.VMEM((2 * n_sp, 256), F32)]),
        out_shape=[jax.ShapeDtypeStruct((bs * t_s, 512), F32), jax.ShapeDtypeStruct((bs * t_s, 2 * n_sp), BF16)],
        compiler_params=_cparams(("parallel", "arbitrary")),
        name="cmp_s",
    )(page_tab.reshape(-1), nsaq, *([cache_t] * g_n), lw["wph"], lw["wpl"])


def _win_s_kernel(q_ref, st_ref, new_ref, o_ref):
    t_s = q_ref.shape[0]
    n_w = st_ref.shape[2]
    rows = H_NSA * t_s
    lane_lo = _iota((1, LANES), 1) < HEAD_DIM
    q = q_ref[...]
    parts = []
    for g in range(HKV_NSA):
        hm = lane_lo if g == 0 else jnp.logical_not(lane_lo)
        for r in range(GQA_R):
            parts.append(jnp.where(hm, q[:, 128 * r:128 * (r + 1)], jnp.zeros((), BF16)))
    qbd = jnp.concatenate(parts, axis=0)
    t_of_row = _iota((rows, 1), 0) % t_s
    jp = _iota((rows, n_w), 1)
    mask_p = (jp > t_of_row + (n_w - WINDOW)) & (jp <= t_of_row + n_w)
    jn = _iota((rows, PAGE), 1)
    mask_n = jn <= t_of_row
    s_p = jnp.where(mask_p, _dot(qbd, st_ref[0, 0:128, :].astype(BF16)), NEG_INF)
    s_n = jnp.where(mask_n, _dot(qbd, new_ref[0, 0:128, :].astype(BF16)), NEG_INF)
    mx = jnp.maximum(jnp.max(s_p, axis=-1, keepdims=True), jnp.max(s_n, axis=-1, keepdims=True))
    p_p = jnp.exp(s_p - mx)
    p_n = jnp.exp(s_n - mx)
    den = jnp.sum(p_p, axis=-1, keepdims=True) + jnp.sum(p_n, axis=-1, keepdims=True)
    o = (_dot_nt(p_p.astype(BF16), st_ref[0, 128:256, :].astype(BF16))
         + _dot_nt(p_n.astype(BF16), new_ref[0, 128:256, :].astype(BF16))) / den
    for r in range(GQA_R):
        o_ref[:, 128 * r:128 * (r + 1)] = jnp.where(
            lane_lo, o[t_s * r:t_s * (r + 1), :], o[t_s * (GQA_R + r):t_s * (GQA_R + r + 1), :])


def _win_s(nsaq, state_t, layer, newkv_t, bs):
    t_s = nsaq.shape[0] // bs
    n_w = state_t.shape[2]
    return pl.pallas_call(
        _win_s_kernel,
        grid=(bs,),
        in_specs=[pl.BlockSpec((t_s, 512), lambda b: (b, 0)),
                  pl.BlockSpec((1, 256, n_w), lambda b: (layer * bs + b, 0, 0)),
                  pl.BlockSpec((1, 256, PAGE), lambda b: (b, 0, 0))],
        out_specs=pl.BlockSpec((t_s, 512), lambda b: (b, 0)),
        out_shape=jax.ShapeDtypeStruct((bs * t_s, 512), F32),
        compiler_params=_cparams(("parallel",)),
        name="win_s",
    )(nsaq, state_t, newkv_t)


def _merge_kernel(x_ref, g_ref, osb_ref, ofox_ref, ocmp_ref, osel_ref, owin_ref, misc_ref,
                  wmg_ref, wsb_ref, wfox_ref, wnsa_ref, wo_ref, ex_ref, y_ref):
    x = x_ref[...]
    h = x * lax.rsqrt(jnp.mean(x * x, axis=-1, keepdims=True) + EPS) * g_ref[...]
    hb = h.astype(BF16)
    mhi, mlo = _split2(misc_ref[...])

    def gate(i):
        ex = ex_ref[:, 512 * i:512 * (i + 1)]
        return _dot(mhi, ex) + _dot(mlo, ex)

    o_nsa = gate(0) * ocmp_ref[...] + gate(1) * osel_ref[...] + gate(2) * owin_ref[...]
    d = x.shape[1]

    def mg(i):
        return jax.nn.sigmoid(_dot_nt(hb, wmg_ref[d * i:d * (i + 1), :]))

    mixed = (mg(0) * _dot(osb_ref[...].astype(BF16), wsb_ref[...])
             + mg(1) * _dot(ofox_ref[...].astype(BF16), wfox_ref[...])
             + mg(2) * _dot(o_nsa.astype(BF16), wnsa_ref[...]))
    y_ref[...] = x + _dot(mixed.astype(BF16), wo_ref[...])


def _merge(x2, o_sb, o_fox, o_cmp, o_sel, o_win, misc, lw, tm):
    n, d = x2.shape
    row = lambda i: (i, 0)
    const = lambda i: (0, 0)
    return pl.pallas_call(
        _merge_kernel,
        grid=(n // tm,),
        in_specs=[pl.BlockSpec((tm, d), row), pl.BlockSpec((1, d), const),
                  pl.BlockSpec((tm, 256), row), pl.BlockSpec((tm, 256), row),
                  pl.BlockSpec((tm, 512), row), pl.BlockSpec((tm, 512), row), pl.BlockSpec((tm, 512), row),
                  pl.BlockSpec((tm, 128), row),
                  pl.BlockSpec((N_MG, d), const), pl.BlockSpec((256, d), const), pl.BlockSpec((256, d), const),
                  pl.BlockSpec((512, d), const), pl.BlockSpec((d, d), const), pl.BlockSpec((128, 1536), const)],
        out_specs=pl.BlockSpec((tm, d), row),
        out_shape=jax.ShapeDtypeStruct((n, d), F32),
        compiler_params=_cparams(("parallel",)),
        name="merge",
    )(x2, lw["attn_norm"], o_sb, o_fox, o_cmp, o_sel, o_win, misc,
      lw["w_mg"], lw["w_br_sb"], lw["w_br_fox"], lw["w_br_nsa"], lw["w_o"], lw["ex"])


FF_CHUNK = 768


def _ffn_kernel(x_ref, g_ref, hist_ref, wup_ref, cw_ref, cb_ref, wdn_ref, y_ref, cs_ref, prev_ref):
    tm = x_ref.shape[0]
    d_ff = wdn_ref.shape[0]
    ti = pl.program_id(1)

    @pl.when(ti == 0)
    def _hist():
        prev_ref[...] = hist_ref[0]

    x = x_ref[...]
    h = x * lax.rsqrt(jnp.mean(x * x, axis=-1, keepdims=True) + EPS) * g_ref[...]
    hb = h.astype(BF16)
    rowi = _iota((tm, 1), 0)
    y = x
    for c in range(0, d_ff, FF_CHUNK):
        gate = _dot(hb, wup_ref[:, c:c + FF_CHUNK])
        up = _dot(hb, wup_ref[:, d_ff + c:d_ff + c + FF_CHUNK])
        p6 = prev_ref[6:7, c:c + FF_CHUNK]
        p7 = prev_ref[7:8, c:c + FF_CHUNK]
        g1 = jnp.where(rowi == 0, p7, pltpu.roll(gate, 1, 0))
        g2 = jnp.where(rowi == 0, p6, jnp.where(rowi == 1, p7, pltpu.roll(gate, 2, 0)))
        conv = (cb_ref[:, c:c + FF_CHUNK] + g2 * cw_ref[0:1, c:c + FF_CHUNK]
                + g1 * cw_ref[1:2, c:c + FF_CHUNK] + gate * cw_ref[2:3, c:c + FF_CHUNK])
        act = conv * jax.nn.sigmoid(conv) * up
        y = y + _dot(act.astype(BF16), wdn_ref[c:c + FF_CHUNK, :])
        tail = gate[tm - 8:tm, :]
        prev_ref[:, c:c + FF_CHUNK] = tail
        cs_ref[0, :, c:c + FF_CHUNK] = tail
    y_ref[...] = y


def _ffn(x2, hist, lw, nb, tm):
    n, d = x2.shape
    nt = n // nb // tm
    d_ff = lw["w_down"].shape[0]
    row = lambda b, i: (b * nt + i, 0)
    const = lambda b, i: (0, 0)
    hmap = (lambda b, i: (b, 0, 0)) if hist.shape[0] == nb else (lambda b, i: (0, 0, 0))
    once = dict(pipeline_mode=pl.Buffered(1))
    return pl.pallas_call(
        _ffn_kernel,
        grid=(nb, nt),
        in_specs=[pl.BlockSpec((tm, d), row), pl.BlockSpec((1, d), const), pl.BlockSpec((1, 8, d_ff), hmap),
                  pl.BlockSpec((d, 2 * d_ff), const, **once), pl.BlockSpec((CONV_W, d_ff), const),
                  pl.BlockSpec((1, d_ff), const), pl.BlockSpec((d_ff, d), const, **once)],
        out_specs=[pl.BlockSpec((tm, d), row), pl.BlockSpec((1, 8, d_ff), lambda b, i: (b, 0, 0))],
        out_shape=[jax.ShapeDtypeStruct((n, d), F32), jax.ShapeDtypeStruct((nb, 8, d_ff), F32)],
        scratch_shapes=[pltpu.VMEM((8, d_ff), F32)],
        compiler_params=_cparams(("parallel", "arbitrary")),
        name="ffn",
    )(x2, lw["ffn_norm"], hist, lw["w_up"], lw["conv_w"], lw["conv_b"], lw["w_down"])


def _consts():
    bd = (np.arange(256)[:, None] // HEAD_DIM == np.arange(256)[None, :] // HEAD_DIM)
    tri_rows = np.arange(256)[:, None] < np.arange(256)[None, :]
    tri128 = np.arange(PAGE)[:, None] > np.arange(PAGE)[None, :]
    hd, ps = np.arange(512) // 128, np.arange(512) % 128
    same = hd[:, None] == hd[None, :]
    c_r = same & (ps[:, None] > ps[None, :])
    c_w = same & (ps[:, None] <= ps[None, :])
    cmat = np.concatenate([c_r, c_w, same], axis=1)
    ex = np.zeros((128, 1536), np.float32)
    for br in range(3):
        for g in range(HKV_NSA):
            for r in range(GQA_R):
                m = 2 * r + g
                ex[H_FOX + br * 8 + g * GQA_R + r, 512 * br + 64 * m:512 * br + 64 * (m + 1)] = 1.0
    as_bf = lambda a: jnp.asarray(a.astype(np.float32), BF16)
    return dict(bd=as_bf(bd), tri_rows=as_bf(tri_rows), tri128=as_bf(tri128), cmat=as_bf(cmat), ex=as_bf(ex))


def _pool_matrices(pool):
    n_pos = G_PAGES * PAGE
    pos = np.arange(n_pos)
    blk = pos // CMP_BLOCK
    col = blk // 2 + (n_pos // SEL_BLOCK) * (blk % 2)
    w = jnp.zeros((n_pos, 128), F32)
    for kv in range(2):
        for g in range(HKV_NSA):
            r = 2 * kv + g
            place = jnp.asarray((32 * r + col[:, None] == np.arange(128)[None, :]).astype(np.float32))
            w = w + place * pool[kv, :, g][pos % CMP_BLOCK][:, None]
    hi = w.astype(BF16)
    lo = (w - hi.astype(F32)).astype(BF16)
    return hi, lo


def _layer_weights(l, cst, attn_norm, w_in, fox_forget_bias, fox_qk_gain, nsa_qk_gain, nsa_cmp_pool,
                   w_br_sb, w_br_fox, w_br_nsa, w_o, ffn_norm, w_up, conv_w, conv_b, w_down):
    w = w_in[l].T
    d = w.shape[1]
    o_foxf = 6 * 256
    o_nsaq = o_foxf + H_FOX
    o_kv = o_nsaq + H_NSA * HEAD_DIM
    o_gate = o_kv + 6 * 128
    o_mg = o_gate + 3 * H_NSA
    head_order = [(m % 2) * GQA_R + m // 2 for m in range(H_NSA)]
    nsaq_rows = jnp.concatenate([w[o_nsaq + 64 * hh:o_nsaq + 64 * (hh + 1)] for hh in head_order], axis=0)
    misc = jnp.concatenate([w[o_foxf:o_foxf + H_FOX], w[o_gate:o_gate + 3 * H_NSA],
                            jnp.zeros((128 - H_FOX - 3 * H_NSA, d), w.dtype)], axis=0)
    w_a = jnp.concatenate([w[0:o_foxf], nsaq_rows, w[o_kv:o_gate], misc], axis=0).astype(BF16)
    tile = lambda v, n: jnp.tile(v, n)[None, :]
    gains = jnp.concatenate([
        tile(fox_qk_gain[l, 0], 4), tile(fox_qk_gain[l, 1], 4), tile(nsa_qk_gain[l, 0], 4),
        tile(nsa_qk_gain[l, 1], 4), tile(nsa_qk_gain[l, 2], 4), tile(nsa_qk_gain[l, 3], 4),
        jnp.zeros((2, 256), F32)], axis=0)
    fbias = jnp.concatenate([fox_forget_bias[l], jnp.zeros((128 - H_FOX,), F32)])[None, :]
    pool = nsa_cmp_pool[l]
    lanes = lambda p: jnp.repeat(p, HEAD_DIM, axis=1)
    pw = jnp.concatenate([lanes(pool[0]), lanes(pool[1])], axis=1)
    zero = jnp.zeros_like(pw)
    wph, wpl = _pool_matrices(pool)
    nsa_rows = jnp.concatenate([w_br_nsa[l, 64 * hh:64 * (hh + 1)] for hh in head_order], axis=0)
    return dict(
        attn_norm=attn_norm[l][None, :], w_a=w_a, w_mg=w[o_mg:].astype(BF16), bd=cst["bd"], gains=gains,
        fbias=fbias, pwe=jnp.concatenate([pw, zero], axis=0), pwo=jnp.concatenate([zero, pw], axis=0),
        wph=wph, wpl=wpl,
        w_br_sb=w_br_sb[l].astype(BF16), w_br_fox=w_br_fox[l].astype(BF16), w_br_nsa=nsa_rows.astype(BF16),
        w_o=w_o[l].astype(BF16), ex=cst["ex"], ffn_norm=ffn_norm[l][None, :], w_up=w_up[l].astype(BF16),
        conv_w=conv_w[l], conv_b=conv_b[l][None, :], w_down=w_down[l].astype(BF16))


def _rope_tables(pos):
    half = ROPE_DIM // 2
    inv = ROPE_THETA ** (-jnp.arange(half, dtype=F32) * (2.0 / ROPE_DIM))
    ang = pos.astype(F32)[:, None] * inv[None, :]
    cos, sin = jnp.cos(ang), jnp.sin(ang)
    t = pos.shape[0]
    one = jnp.ones((t, HEAD_DIM - ROPE_DIM), F32)
    zero8 = jnp.zeros((t, half), F32)
    zero48 = jnp.zeros((t, HEAD_DIM - ROPE_DIM), F32)
    two = lambda a: jnp.concatenate([a, a], axis=1)
    return (two(jnp.concatenate([cos, cos, one], axis=1)),
            two(jnp.concatenate([-sin, zero8, zero48], axis=1)),
            two(jnp.concatenate([zero8, sin, zero48], axis=1)))


def _new_block_t(a, bs):
    t = a.shape[0] // bs
    at = jnp.swapaxes(a.reshape(bs, t, a.shape[1]), 1, 2)
    return jnp.pad(at, ((0, 0), (0, 0), (0, PAGE - t)))


def _feature_major(c):
    depth, n, npos = c.shape[0:3]
    ct = jnp.moveaxis(c, 2, -1)
    return ct.reshape(depth * n, -1, npos)


def _prompt_layer(x2, b, t, lw, cst, tabs):
    tm = 512 if t % 512 == 0 else 256
    (sbk, foxk, nsak, nsaq, cmpkv, misc,
     sbqt, foxqt, nsaqt, sbkvt, foxkvt, cmpt, selt, wint, lft) = _proj(x2, lw, *tabs, tm, nb=b)
    tq = 512 if t % 512 == 0 else 256
    o_sb = _attn("sb", sbqt, sbk, sbkvt, b, t, tri=cst["tri_rows"], tq=tq)
    o_fox = _attn("fox", foxqt, foxk, foxkvt, b, t, extra=_fox_bias(misc, cst["tri_rows"], tq), tq=tq)
    o_cmp, selm = _cmp_p(nsaq, cmpkv, lw, b, t)
    o_sel = _attn("sel", nsaqt, nsak, selt, b, t, kcol=1, extra=selm, tq=tq)
    o_win = _attn("win", nsaqt, nsak, wint, b, t, kcol=2, tq=256)
    x_mid = _merge(x2, o_sb, o_fox, o_cmp, o_sel, o_win, misc, lw, 256)
    d_ff = lw["w_down"].shape[0]
    y, cs = _ffn(x_mid, jnp.zeros((1, 8, d_ff), F32), lw, b, 256)
    n_w = min(WINDOW, t)
    state = (sbkvt, foxkvt, lft[:, 0:H_FOX], cmpt, selt, wint[:, :, t - n_w:], cs[:, 8 - (CONV_W - 1):])
    return y, state


def _sample_layer(x2, bs, l, lw, cst, tabs, page_tab, c_sb, c_fox, c_logf, c_cmp, c_sel, st_win, st_conv):
    n = x2.shape[0]
    t_s = n // bs
    sbq, sbkv, foxq, foxkv, nsaq, cmpkv, selkv, winkv, misc = _proj(x2, lw, *tabs, n)
    logf_new = misc[:, 0:H_FOX]
    lf_new_pages = _new_block_t(logf_new, bs).reshape(bs, PAGE * H_FOX)
    rt_new = _logf_sums(lf_new_pages, cst["cmat"]).reshape(bs, 1, 1536)
    n_pool = c_logf.shape[0] // 2
    rt_cache = _logf_sums(c_logf[l * n_pool:(l + 1) * n_pool].reshape(n_pool, PAGE * H_FOX), cst["cmat"])
    rt_cache = rt_cache.reshape(n_pool, 1, 1536)
    o_sb = _dec("sb", sbq, _new_block_t(sbkv, bs), c_sb, l, page_tab, cst["tri128"])
    o_fox = _dec("fox", foxq, _new_block_t(foxkv, bs), c_fox, l, page_tab, rt_new=rt_new, rt_cache=rt_cache)
    o_cmp, selm = _cmp_s(nsaq, c_cmp, l, page_tab, lw)
    o_sel = _dec("sel", nsaq, _new_block_t(selkv, bs), c_sel, l, page_tab, selm)
    o_win = _win_s(nsaq, st_win, l, _new_block_t(winkv, bs), bs)
    x_mid = _merge(x2, o_sb, o_fox, o_cmp, o_sel, o_win, misc, lw, n)
    hist = jnp.pad(st_conv, ((0, 0), (8 - (CONV_W - 1), 0), (0, 0)))
    y, cs = _ffn(x_mid, hist, lw, bs, t_s)
    state = (sbkv, foxkv, logf_new, cmpkv, selkv, winkv, cs[:, 8 - (CONV_W - 1):])
    return y, state


def kernel(x_prompt, x_sample, cache_sb_kv, cache_fox_kv, cache_fox_logf, cache_nsa_cmp_kv, cache_nsa_sel_kv, state_nsa_win_kv, state_ffn_conv, page_table, attn_norm, w_in, fox_forget_bias, fox_qk_gain, nsa_qk_gain, nsa_cmp_pool, w_br_sb, w_br_fox, w_br_nsa, w_o, ffn_norm, w_up, conv_w, conv_b, w_down):
    b, t, d = x_prompt.shape
    bs, t_s, _ = x_sample.shape
    depth = w_in.shape[0]
    n_pages = page_table.shape[1]
    past = n_pages * PAGE
    n_w = state_nsa_win_kv.shape[2]
    assert t % 256 == 0 and t_s == 8 and n_pages % G_DEC == 0 and past >= WINDOW and depth == 2
    cst = _consts()
    tabs_p = _rope_tables(jnp.arange(t, dtype=jnp.int32))
    tabs_s = tuple(jnp.tile(a, (bs, 1)) for a in _rope_tables(past + jnp.arange(t_s, dtype=jnp.int32)))
    c_sb, c_fox = _feature_major(cache_sb_kv), _feature_major(cache_fox_kv)
    c_cmp, c_sel = _feature_major(cache_nsa_cmp_kv), _feature_major(cache_nsa_sel_kv)
    c_logf = _feature_major(cache_fox_logf)
    st_win = _feature_major(state_nsa_win_kv)
    hp = x_prompt.reshape(b * t, d)
    hs = x_sample.reshape(bs * t_s, d)
    st_p, st_s = [], []
    for l in range(depth):
        lw = _layer_weights(l, cst, attn_norm, w_in, fox_forget_bias, fox_qk_gain, nsa_qk_gain, nsa_cmp_pool,
                            w_br_sb, w_br_fox, w_br_nsa, w_o, ffn_norm, w_up, conv_w, conv_b, w_down)
        hp, new_p = _prompt_layer(hp, b, t, lw, cst, tabs_p)
        hs, new_s = _sample_layer(hs, bs, l, lw, cst, tabs_s, page_table, c_sb, c_fox, c_logf, c_cmp, c_sel,
                                  st_win, state_ffn_conv[l])
        st_p.append(new_p)
        st_s.append(new_s)

    def stack_p(i, shape):
        a = jnp.stack([s[i] for s in st_p], axis=0)
        a = a.reshape((depth, b) + shape + (a.shape[-1],))
        return jnp.moveaxis(a, -1, 2)

    def stack_s(i, shape):
        return jnp.stack([s[i].reshape((bs, t_s) + shape) for s in st_s], axis=0)

    kv = lambda h: (2, h, HEAD_DIM)
    win_new = stack_s(5, kv(HKV_NSA))
    win_s = jnp.concatenate([state_nsa_win_kv[:, :, t_s:], win_new], axis=2)[:, :, -n_w:]
    return (hp.reshape(b, t, d), hs.reshape(bs, t_s, d),
            stack_p(0, kv(H_SB)), stack_s(0, kv(H_SB)),
            stack_p(1, kv(H_FOX)), stack_s(1, kv(H_FOX)),
            stack_p(2, (H_FOX,)), stack_s(2, (H_FOX,)),
            stack_p(3, kv(HKV_NSA)), stack_s(3, kv(HKV_NSA)),
            stack_p(4, kv(HKV_NSA)), stack_s(4, kv(HKV_NSA)),
            stack_p(5, kv(HKV_NSA)), win_s,
            jnp.stack([s[6] for s in st_p], axis=0), jnp.stack([s[6] for s in st_s], axis=0))
```

```python
import functools

import numpy as np
import jax
import jax.numpy as jnp
from jax import lax
from jax.experimental import pallas as pl
from jax.experimental.pallas import tpu as pltpu

F32 = jnp.float32
BF16 = jnp.bfloat16

HEAD_DIM = 64
H_SB = 4
H_FOX = 4
H_NSA = 8
HKV_NSA = 2
GQA_R = H_NSA // HKV_NSA
ROPE_DIM = HEAD_DIM // 4
ROPE_THETA = 500000.0
CMP_BLOCK = 32
SEL_BLOCK = 64
SEL_TOPK = 16
WINDOW = 512
CONV_W = 3
EPS = 1e-6
SCALE = HEAD_DIM ** -0.5
NEG_INF = -1e30
TINY = 1e-30
FORCE_SCORE = 1e6
PAGE = 128
LANES = 128
VMEM_LIMIT = 56 * 1024 * 1024
SB_DEAD = 110.0

C_SBQ, C_SBKV, C_FOXQ, C_FOXK, C_FOXV = 0, 256, 768, 1024, 1280
C_NSAQ, C_CMP, C_SEL, C_WIN, C_MISC, C_END = 1536, 2048, 2304, 2560, 2816, 2944
N_MG = 3 * 1024


def _dot(a, b):
    return jnp.dot(a, b, preferred_element_type=F32)


def _dot_nt(a, b):
    return lax.dot_general(a, b, (((1,), (1,)), ((), ())), preferred_element_type=F32)


def _split2(x):
    hi = x.astype(BF16)
    lo = (x - hi.astype(F32)).astype(BF16)
    return hi, lo


def _split3(x):
    hi = x.astype(BF16)
    r = x - hi.astype(F32)
    mid = r.astype(BF16)
    lo = (r - mid.astype(F32)).astype(BF16)
    return hi, mid, lo


def _softplus(z):
    return jnp.maximum(z, 0.0) + jnp.log(1.0 + jnp.exp(-jnp.abs(z)))


def _iota(shape, dim):
    return lax.broadcasted_iota(jnp.int32, shape, dim)


def _cparams(sem):
    return pltpu.CompilerParams(dimension_semantics=sem, vmem_limit_bytes=VMEM_LIMIT)


def _proj_kernel(feature_major, x_ref, g_ref, w_ref, bd_ref, gains_ref, cos_ref, sa_ref, sb_ref, fb_ref, *outs):
    x = x_ref[...]
    h = x * lax.rsqrt(jnp.mean(x * x, axis=-1, keepdims=True) + EPS) * g_ref[...]
    hb = h.astype(BF16)

    def seg(lo, n):
        return _dot_nt(hb, w_ref[lo:lo + n, :])

    def hrms(z, gain):
        w = z.shape[1]
        hi, lo = _split2(z * z)
        bd = bd_ref[0:w, 0:w]
        ss = _dot(hi, bd) + _dot(lo, bd)
        return z * lax.rsqrt(ss * (1.0 / HEAD_DIM) + EPS) * gain

    def rope(z):
        return (z * cos_ref[...] + pltpu.roll(z, LANES - ROPE_DIM // 2, 1) * sa_ref[...]
                + pltpu.roll(z, ROPE_DIM // 2, 1) * sb_ref[...])

    sb_q = seg(C_SBQ, 256) * SCALE
    sb_kv = seg(C_SBKV, 512)
    fox_q = hrms(seg(C_FOXQ, 256), gains_ref[0:1, :]) * SCALE
    fox_k = hrms(seg(C_FOXK, 256), gains_ref[1:2, :])
    fox_v = seg(C_FOXV, 256)
    nsa_q = []
    for c in range(2):
        zq = hrms(seg(C_NSAQ + 256 * c, 256), gains_ref[2:3, :])
        nsa_q += [rope(zq[:, 128 * s:128 * (s + 1)]) * SCALE for s in range(2)]
    nsa_k, nsa_v = [], []
    for i, col in enumerate((C_CMP, C_SEL, C_WIN)):
        nsa_k.append(rope(hrms(seg(col, 128), gains_ref[3 + i:4 + i, 0:128])))
        nsa_v.append(seg(col + 128, 128))
    zm = seg(C_MISC, 128)
    lane = _iota((1, LANES), 1)
    misc = jnp.where(lane < H_FOX, -_softplus(-(zm + fb_ref[...])), jax.nn.sigmoid(zm))

    if not feature_major:
        sbq_ref, sbkv_ref, foxq_ref, foxkv_ref, nsaq_ref, cmp_ref, sel_ref, win_ref, misc_ref = outs
        sbq_ref[...] = sb_q.astype(BF16)
        sbkv_ref[...] = sb_kv
        foxq_ref[...] = fox_q.astype(BF16)
        foxkv_ref[:, 0:256] = fox_k
        foxkv_ref[:, 256:512] = fox_v
        for s in range(4):
            nsaq_ref[:, 128 * s:128 * (s + 1)] = nsa_q[s].astype(BF16)
        for i, ref in enumerate((cmp_ref, sel_ref, win_ref)):
            ref[:, 0:128] = nsa_k[i]
            ref[:, 128:256] = nsa_v[i]
        misc_ref[...] = misc
        return

    (sbk_ref, foxk_ref, nsak_ref, nsaq_ref, cmpkv_ref, misc_ref,
     sbqt_ref, foxqt_ref, nsaqt_ref, sbkvt_ref, foxkvt_ref, cmpt_ref, selt_ref, wint_ref, lft_ref) = outs
    sbk_ref[...] = sb_kv[:, 0:256].astype(BF16)
    foxk_ref[...] = fox_k.astype(BF16)
    for i in range(3):
        nsak_ref[:, 128 * i:128 * (i + 1)] = nsa_k[i].astype(BF16)
    for s in range(4):
        nsaq_ref[:, 128 * s:128 * (s + 1)] = nsa_q[s].astype(BF16)
        nsaqt_ref[0, 128 * s:128 * (s + 1), :] = nsa_q[s].T.astype(BF16)
    cmpkv_ref[:, 0:128] = nsa_k[0]
    cmpkv_ref[:, 128:256] = nsa_v[0]
    misc_ref[...] = misc
    sbqt_ref[0] = sb_q.T.astype(BF16)
    foxqt_ref[0] = fox_q.T.astype(BF16)
    sbkvt_ref[0] = sb_kv.T
    foxkvt_ref[0, 0:256, :] = fox_k.T
    foxkvt_ref[0, 256:512, :] = fox_v.T
    for i, ref in enumerate((cmpt_ref, selt_ref, wint_ref)):
        ref[0, 0:128, :] = nsa_k[i].T
        ref[0, 128:256, :] = nsa_v[i].T
    lft_ref[0] = misc.T[0:8, :]


def _proj(x2, lw, cos_t, sa_t, sb_t, tm, nb=None):
    n, d = x2.shape
    nt = cos_t.shape[0] // tm
    row = lambda i: (i, 0)
    const = lambda i: (0, 0)
    tab = pl.BlockSpec((tm, LANES), lambda i: (i % nt, 0))
    if nb is None:
        outs = [(256, BF16), (512, F32), (256, BF16), (512, F32), (512, BF16),
                (256, F32), (256, F32), (256, F32), (128, F32)]
        out_specs = [pl.BlockSpec((tm, w), row) for w, _ in outs]
        out_shape = [jax.ShapeDtypeStruct((n, w), dt) for w, dt in outs]
    else:
        t = n // nb
        nat = [(256, BF16), (256, BF16), (384, BF16), (512, BF16), (256, F32), (128, F32)]
        fm = [(256, BF16), (256, BF16), (512, BF16), (512, F32), (512, F32), (256, F32), (256, F32), (256, F32),
              (8, F32)]
        fmap = lambda i: (i // nt, 0, i % nt)
        out_specs = ([pl.BlockSpec((tm, w), row) for w, _ in nat]
                     + [pl.BlockSpec((1, w, tm), fmap) for w, _ in fm])
        out_shape = ([jax.ShapeDtypeStruct((n, w), dt) for w, dt in nat]
                     + [jax.ShapeDtypeStruct((nb, w, t), dt) for w, dt in fm])
    return pl.pallas_call(
        functools.partial(_proj_kernel, nb is not None),
        grid=(n // tm,),
        in_specs=[pl.BlockSpec((tm, d), row), pl.BlockSpec((1, d), const),
                  pl.BlockSpec((C_END, d), const), pl.BlockSpec((256, 256), const),
                  pl.BlockSpec((8, 256), const), tab, tab, tab, pl.BlockSpec((1, LANES), const)],
        out_specs=out_specs,
        out_shape=out_shape,
        compiler_params=_cparams(("parallel",)),
        name="proj",
    )(x2, lw["attn_norm"], lw["w_a"], lw["bd"], lw["gains"], cos_t, sa_t, sb_t, lw["fbias"])


def _logf_kernel(x_ref, c_ref, o_ref):
    parts = _split3(x_ref[...])
    c = c_ref[...]
    o_ref[...] = _dot(parts[0], c) + _dot(parts[1], c) + _dot(parts[2], c)


def _logf_sums(lf_pages, cmat):
    p = lf_pages.shape[0]
    tp = 256 if p % 256 == 0 else p
    return pl.pallas_call(
        _logf_kernel,
        grid=(p // tp,),
        in_specs=[pl.BlockSpec((tp, 512), lambda i: (i, 0)), pl.BlockSpec((512, 1536), lambda i: (0, 0))],
        out_specs=pl.BlockSpec((tp, 1536), lambda i: (i, 0)),
        out_shape=jax.ShapeDtypeStruct((p, 1536), F32),
        compiler_params=_cparams(("parallel",)),
        name="logf",
    )(lf_pages, cmat)


def _fox_bias_kernel(lf_ref, tri_ref, aug_ref, tt_ref):
    tk = lf_ref.shape[0]
    lf = lf_ref[...]
    tri = tri_ref[...]
    p3 = _split3(lf)
    rb = _dot(tri, p3[0]) + _dot(tri, p3[1]) + _dot(tri, p3[2])
    hi = rb.astype(BF16).astype(F32)
    r1 = rb - hi
    mid = r1.astype(BF16).astype(F32)
    lo = (r1 - mid).astype(BF16).astype(F32)
    li = _iota((tk, LANES), 1)
    comb = jnp.where(li < 20, pltpu.roll(hi, 16, 1), jnp.where(li < 24, pltpu.roll(mid, 20, 1), pltpu.roll(lo, 24, 1)))
    comb = jnp.where((li >= 16) & (li < 28), comb, 0.0)
    aug_ref[...] = jnp.where(li < 3, 1.0, comb).astype(BF16)
    tot = rb[0:1, :] + lf[0:1, :]
    diag = _iota((8, LANES), 0) == _iota((8, LANES), 1)
    tt_ref[0] = jnp.broadcast_to(jnp.sum(jnp.where(diag, tot, 0.0), axis=1, keepdims=True), tt_ref.shape[1:])


def _fox_bias(misc, tri, tq, tk=256):
    n = misc.shape[0]
    return pl.pallas_call(
        _fox_bias_kernel,
        grid=(n // tk,),
        in_specs=[pl.BlockSpec((tk, 128), lambda i: (i, 0)), pl.BlockSpec((tk, tk), lambda i: (0, 0))],
        out_specs=[pl.BlockSpec((tk, 128), lambda i: (i, 0)), pl.BlockSpec((1, 8, tq), lambda i: (i, 0, 0))],
        out_shape=[jax.ShapeDtypeStruct((n, 128), BF16), jax.ShapeDtypeStruct((n // tk, 8, tq), F32)],
        compiler_params=_cparams(("parallel",)),
        name="fox_bias",
    )(misc, tri)


def _attn_kernel(mode, nsa, tq, tk, n_s, qi_ref, kb_ref, *refs):
    refs = list(refs)
    qt_ref, k_ref, vt_ref = refs[0:3]
    pos = 3
    tri_ref = lf_ref = tt_ref = sm_ref = None
    if mode == "sb":
        tri_ref = refs[pos]; pos += 1
    if mode == "fox":
        lf_ref, tt_ref = refs[pos], refs[pos + 1]; pos += 2
    if mode == "sel":
        sm_ref = refs[pos]; pos += 1
    o_ref, acc_ref, m_ref = refs[pos:pos + 3]
    cf_ref = refs[pos + 3] if mode == "fox" else None
    dead_ref = refs[pos + 3] if mode == "sb" else None
    nh = H_NSA if nsa else H_SB

    p_id = pl.program_id(1)
    qi = qi_ref[p_id]
    kb = kb_ref[p_id]
    ratio = tq // tk
    first = kb == (qi + 1) * ratio - 1
    w_blocks = WINDOW // tk
    last = (kb == jnp.maximum(qi * ratio - w_blocks, 0)) if mode == "win" else (kb == 0)

    @pl.when(first)
    def _init():
        acc_ref[...] = jnp.zeros_like(acc_ref)
        m_ref[...] = jnp.zeros_like(m_ref) if mode == "sb" else jnp.full_like(m_ref, NEG_INF)
        if mode == "fox":
            cf_ref[...] = jnp.zeros_like(cf_ref)
        if mode == "sb":
            dead_ref[0] = 0

    row_lo = _iota((LANES, 1), 0) < HEAD_DIM

    def body(edge):
        mask = None
        if edge:
            d = (qi * tq + _iota((tk, tq), 1)) - (kb * tk + _iota((tk, tq), 0))
            if mode == "sb":
                mask = d > 0
            elif mode == "win":
                mask = (d >= 0) & (d < WINDOW)
            else:
                mask = d >= 0
        cf = None
        if mode == "fox":
            cf = cf_ref[...]
            cf_ref[...] = cf + tt_ref[0]

        hms, vts, qtms, kss = [], [], [], []
        for n in range(nh):
            slab, sub = divmod(n, 2)
            ko = 0 if nsa else 128 * slab
            hm = row_lo if sub == 0 else jnp.logical_not(row_lo)
            qtms.append(jnp.where(hm, qt_ref[128 * slab:128 * (slab + 1), :], jnp.zeros((), BF16)))
            kss.append(k_ref[:, ko:ko + 128])
            vo = 0 if nsa else 128 * slab
            vts.append(vt_ref[0, 0, vo:vo + 128, :].astype(BF16))
            hms.append(hm)
        if mode == "sb":
            ss = [_dot(kss[n], qtms[n]) for n in range(nh)]
            tri = tri_ref[...]
            sps, spms, hls = [], [], []
            for n in range(nh):
                sp = _softplus(ss[n])
                spm = sp if mask is None else jnp.where(mask, sp, 0.0)
                sps.append(sp)
                spms.append(spm)
                hls.append(_split2(spm))
            als = [_dot(tri, hi) + _dot(tri, lo) for hi, lo in hls]
            avs = []
            for n in range(nh):
                carry = m_ref[n]
                a = jnp.exp((ss[n] - sps[n]) - (als[n] + carry[0:1, :]))
                if mask is not None:
                    a = jnp.where(mask, a, 0.0)
                avs.append(a.astype(BF16))
                m_ref[n] = carry + (als[n][0:1, :] + spms[n][0:1, :])
            for n in range(nh):
                acc_ref[n] += _dot(vts[n], avs[n])
            dead_ref[0] = (jnp.min(m_ref[...]) > SB_DEAD).astype(jnp.int32)
        else:
            li = _iota((tk, LANES), 1)
            ones3 = li < 3
            ri16 = _iota((16, 1), 0)
            zeros16 = jnp.zeros((16, tq), BF16)
            if mode == "sel":
                smb = ((sm_ref[0].astype(F32) - 1.0) * 1e30).astype(BF16)
                blk = 16 + kb * (tk // SEL_BLOCK) + _iota((tk, LANES), 0) // SEL_BLOCK
                augl = jnp.where(ones3 | (li == blk), 1.0, 0.0).astype(BF16)
                pad = jnp.zeros((LANES - 16 - n_s, tq), BF16)
                bases = [jnp.concatenate([smb[g * n_s:(g + 1) * n_s], pad], axis=0) for g in range(2)]
                base_of = lambda n: bases[n % 2]
            elif mode == "fox":
                augl = lf_ref[...]
                pad = jnp.zeros((LANES - 32, tq), BF16)
                pick = [jnp.broadcast_to(jnp.where(((ri16 % 4) == n) & (ri16 < 12), 1.0, 0.0), (16, tq))
                        .astype(BF16) for n in range(nh)]
                base_of = lambda n: jnp.concatenate([pick[n], pad], axis=0)
            else:
                augl = jnp.where(ones3, 1.0, 0.0).astype(BF16)
                base_win = jnp.zeros((LANES - 16, tq), BF16)
                base_of = lambda n: base_win
            lhs = [jnp.concatenate([kss[n], augl], axis=1) for n in range(nh)]
            m_news, alphas = [], []
            for n in range(nh):
                s = _dot(lhs[n], jnp.concatenate([qtms[n], zeros16, base_of(n)], axis=0))
                if mask is not None:
                    s = jnp.where(mask, s, NEG_INF)
                m_old = m_ref[n]
                m_blk = jnp.max(s, axis=0, keepdims=True)
                if mode == "fox":
                    m_blk = m_blk + cf[n:n + 1, :]
                m_new = jnp.maximum(m_old, m_blk)
                m_news.append(m_new)
                alphas.append(jnp.exp(m_old - m_new))
                m_ref[n] = m_new
            ps = []
            for n in range(nh):
                shift = -m_news[n][0:1, :]
                if mode == "fox":
                    shift = shift + cf[n:n + 1, :]
                t3 = _split3(shift)
                terms = jnp.where(ri16 == 0, t3[0].astype(F32), jnp.where(
                    ri16 == 1, t3[1].astype(F32), jnp.where(ri16 == 2, t3[2].astype(F32), 0.0))).astype(BF16)
                s = _dot(lhs[n], jnp.concatenate([qtms[n], terms, base_of(n)], axis=0))
                if mask is not None:
                    s = jnp.where(mask, s, NEG_INF)
                ps.append(jnp.exp(s).astype(BF16))
            for n in range(nh):
                vaug = jnp.where(hms[n], vts[n], jnp.ones((), BF16))
                acc_ref[n] = acc_ref[n] * alphas[n][0:1, :] + _dot(vaug, ps[n])

    d_min = qi * tq - kb * tk - (tk - 1)
    d_max = qi * tq + (tq - 1) - kb * tk
    if mode == "win":
        edge = (d_min < 0) | (d_max >= WINDOW)
    elif mode == "sb":
        edge = d_min <= 0
    else:
        edge = d_min < 0
    live = (dead_ref[0] == 0) if mode == "sb" else True
    @pl.when(edge & live)
    def _edge():
        body(True)

    @pl.when(jnp.logical_not(edge) & live)
    def _inner():
        body(False)

    @pl.when(last)
    def _fin():
        for slab in range(nh // 2):
            a0, a1 = acc_ref[2 * slab], acc_ref[2 * slab + 1]
            if mode == "sb":
                o0, o1 = a0[0:HEAD_DIM, :], a1[HEAD_DIM:LANES, :]
            else:
                o0 = a0[0:HEAD_DIM, :] / a0[HEAD_DIM:HEAD_DIM + 1, :]
                o1 = a1[HEAD_DIM:LANES, :] / a1[0:1, :]
            o_ref[:, 128 * slab:128 * (slab + 1)] = jnp.concatenate([o0, o1], axis=0).T


def _pairs(nq, mode, tq, tk):
    ratio = tq // tk
    qi, kb = [], []
    for i in range(nq):
        lo = max(i * ratio - WINDOW // tk, 0) if mode == "win" else 0
        for k in range((i + 1) * ratio - 1, lo - 1, -1):
            qi.append(i)
            kb.append(k)
    return jnp.asarray(qi, jnp.int32), jnp.asarray(kb, jnp.int32)


def _attn(mode, qt, k, kvt, b, t, kcol=0, extra=None, tri=None, tq=512, tk=256):
    nsa = mode in ("sel", "win")
    wq = qt.shape[1]
    wv = kvt.shape[1] // 2
    wk = 128 if nsa else 256
    nq = t // tq
    n_s = t // SEL_BLOCK
    qi_tab, kb_tab = _pairs(nq, mode, tq, tk)
    in_specs = [pl.BlockSpec((1, wq, tq), lambda bi, p, qi, kb: (bi, 0, qi[p])),
                pl.BlockSpec((tk, wk), lambda bi, p, qi, kb: (bi * (t // tk) + kb[p], kcol)),
                pl.BlockSpec((1, 1, wv, tk), lambda bi, p, qi, kb: (bi, 1, 0, kb[p]))]
    args = [qt, k, kvt.reshape(b, 2, wv, t)]
    if mode == "sb":
        in_specs.append(pl.BlockSpec((tk, tk), lambda bi, p, qi, kb: (0, 0)))
        args.append(tri)
    if mode == "fox":
        in_specs.append(pl.BlockSpec((tk, 128), lambda bi, p, qi, kb: (bi * (t // tk) + kb[p], 0)))
        in_specs.append(pl.BlockSpec((1, 8, tq), lambda bi, p, qi, kb: (bi * (t // tk) + kb[p], 0, 0)))
        args += list(extra)
    if mode == "sel":
        in_specs.append(pl.BlockSpec((1, 2 * n_s, tq), lambda bi, p, qi, kb: (bi, 0, qi[p])))
        args.append(extra)
    nh = H_NSA if nsa else H_SB
    scratch = [pltpu.VMEM((nh, 128, tq), F32), pltpu.VMEM((nh, 8, tq), F32)]
    if mode == "fox":
        scratch.append(pltpu.VMEM((8, tq), F32))
    if mode == "sb":
        scratch.append(pltpu.SMEM((1,), jnp.int32))

    def kernel(qi_ref, kb_ref, qt_ref, *rest):
        _attn_kernel(mode, nsa, tq, tk, n_s, qi_ref, kb_ref, qt_ref.at[0], *rest)

    return pl.pallas_call(
        kernel,
        grid_spec=pltpu.PrefetchScalarGridSpec(
            num_scalar_prefetch=2, grid=(b, int(qi_tab.shape[0])),
            in_specs=in_specs, out_specs=pl.BlockSpec((tq, wq), lambda bi, p, qi, kb: (bi * nq + qi[p], 0)),
            scratch_shapes=scratch),
        out_shape=jax.ShapeDtypeStruct((b * t, wq), F32),
        compiler_params=_cparams(("parallel", "arbitrary")),
        name="attn_" + mode,
    )(qi_tab, kb_tab, *args)


def _pool_blocks(x, pwe_ref, pwo_ref):
    n = x.shape[0] // SEL_BLOCK
    x3 = x.reshape(n, SEL_BLOCK, 256)
    return (jnp.sum(x3 * pwe_ref[...][None], axis=1), jnp.sum(x3 * pwo_ref[...][None], axis=1))


def _rank_select(sc, n_blk, k_sel):
    j = _iota((n_blk, 1), 0)
    rank = jnp.zeros(sc.shape, F32)
    for i in range(n_blk):
        ri = sc[i:i + 1, :]
        beats = (ri > sc) | ((ri == sc) & (j > i))
        rank = rank + beats.astype(F32)
    return (rank < k_sel).astype(F32)


def _cmp_p_kernel(tq, t, q_ref, ckv_ref, pwe_ref, pwo_ref, o_ref, sm_ref, kc_ref):
    n_s = t // SEL_BLOCK
    n_c = 2 * n_s
    qi = pl.program_id(1)

    @pl.when(qi == 0)
    def _pool():
        ev, od = _pool_blocks(ckv_ref[...], pwe_ref, pwo_ref)
        kc_ref[0:n_s, :] = ev
        kc_ref[n_s:n_c, :] = od

    kcb = kc_ref[:, 0:128].astype(BF16)
    vcb = kc_ref[:, 128:256].astype(BF16)

    def c_end(i):
        blk = jnp.where(i < n_s, 2 * i, 2 * (i - n_s) + 1)
        return (blk + 1) * CMP_BLOCK - 1

    cmask = c_end(_iota((1, n_c), 1)) <= qi * tq + _iota((tq, 1), 0)
    t_row = qi * tq + _iota((1, tq), 1)
    cmask_t = c_end(_iota((n_c, 1), 0)) <= t_row
    lane_lo = _iota((1, LANES), 1) < HEAD_DIM
    imp = [jnp.zeros((n_s, tq), F32), jnp.zeros((n_s, tq), F32)]
    outs = {}
    for n in range(H_NSA):
        r, g = divmod(n, 2)
        hm = lane_lo if g == 0 else jnp.logical_not(lane_lo)
        qh = jnp.where(hm, q_ref[:, 128 * r:128 * (r + 1)], jnp.zeros((), BF16))
        s = jnp.where(cmask, _dot_nt(qh, kcb), NEG_INF)
        e = jnp.where(cmask, jnp.exp(s - jnp.max(s, axis=-1, keepdims=True)), 0.0)
        p = e / jnp.maximum(jnp.sum(e, axis=-1, keepdims=True), TINY)
        outs[(r, g)] = _dot(p.astype(BF16), vcb)
        st = jnp.where(cmask_t, _dot_nt(kcb, qh), NEG_INF)
        et = jnp.where(cmask_t, jnp.exp(st - jnp.max(st, axis=0, keepdims=True)), 0.0)
        pt = et / jnp.maximum(jnp.sum(et, axis=0, keepdims=True), TINY)
        imp[g] = imp[g] + (pt[0:n_s, :] + pt[n_s:n_c, :])
    for r in range(GQA_R):
        o_ref[:, 128 * r:128 * (r + 1)] = jnp.where(lane_lo, outs[(r, 0)], outs[(r, 1)])

    j = _iota((n_s, 1), 0)
    cur = t_row // SEL_BLOCK
    forced = (j == 0) | (j == cur) | (j == cur - 1)
    valid = j * SEL_BLOCK <= t_row
    sel = []
    for g in range(2):
        sc = jnp.where(forced, FORCE_SCORE, jnp.where(valid, imp[g], -1.0))
        sel.append(_rank_select(sc, n_s, min(SEL_TOPK, n_s)))
    sm_ref[0] = jnp.concatenate(sel, axis=0).astype(BF16)


def _cmp_p(nsaq, ckv, lw, b, t, tq=256):
    n = nsaq.shape[0]
    nq = t // tq
    n_s = t // SEL_BLOCK
    rowmap = lambda bi, qi: (bi * nq + qi, 0)
    const = lambda bi, qi: (0, 0)
    return pl.pallas_call(
        functools.partial(_cmp_p_kernel, tq, t),
        grid=(b, nq),
        in_specs=[pl.BlockSpec((tq, 512), rowmap), pl.BlockSpec((t, 256), lambda bi, qi: (bi, 0)),
                  pl.BlockSpec((SEL_BLOCK, 256), const), pl.BlockSpec((SEL_BLOCK, 256), const)],
        out_specs=[pl.BlockSpec((tq, 512), rowmap), pl.BlockSpec((1, 2 * n_s, tq), lambda bi, qi: (bi, 0, qi))],
        out_shape=[jax.ShapeDtypeStruct((n, 512), F32), jax.ShapeDtypeStruct((b, 2 * n_s, t), BF16)],
        scratch_shapes=[pltpu.VMEM((2 * n_s, 256), F32)],
        compiler_params=_cparams(("parallel", "arbitrary")),
        name="cmp_p",
    )(nsaq, ckv, lw["pwe"], lw["pwo"])


G_PAGES = 8
G_DEC = 32


def _dec_kernel(mode, n_pages, n_sp, pt_ref, *refs):
    refs = list(refs)
    nsa = mode == "sel"
    g_n = G_DEC
    q_ref, new_ref = refs[0], refs[1]
    pages = refs[2:2 + g_n]
    pos = 2 + g_n
    tri_ref = sm_ref = rtn_ref = None
    rts = None
    if mode == "sb":
        tri_ref = refs[pos]; pos += 1
    if mode == "fox":
        rtn_ref = refs[pos]; pos += 1
        rts = refs[pos:pos + g_n]; pos += g_n
    if mode == "sel":
        sm_ref = refs[pos]; pos += 1
    o_ref, qbd_ref, acc_ref, m_ref, l_ref = refs[pos:pos + 5]
    cf_ref = refs[pos + 5] if mode == "fox" else None
    dead_ref = refs[pos + 5] if mode == "sb" else None

    t_s = q_ref.shape[0]
    nh = H_NSA if nsa else H_SB
    rows = nh * t_s
    kw = 128 if nsa else 256
    step = pl.program_id(1)
    n_steps = pl.num_programs(1)
    lane_lo = _iota((1, LANES), 1) < HEAD_DIM
    t_of_row = _iota((rows, 1), 0) % t_s

    def head_lanes(n, width):
        ln = _iota((1, width), 1)
        return (ln >= HEAD_DIM * n) & (ln < HEAD_DIM * (n + 1))

    def block(kt32, vt32, mask, bias):
        ktb = kt32.astype(BF16)
        vtb = vt32.astype(BF16)
        s = _dot(qbd_ref[...], ktb)
        if mode == "sb":
            sp = _softplus(s)
            spm = sp if mask is None else jnp.where(mask, sp, 0.0)
            tri = tri_ref[...]
            later = m_ref[...]
            als = []
            for c in reversed(range(s.shape[1] // PAGE)):
                spc = spm[:, PAGE * c:PAGE * (c + 1)]
                hi, lo = _split2(spc)
                alc = _dot(hi, tri) + _dot(lo, tri)
                als.append(alc + later)
                later = later + alc[:, 0:1] + spc[:, 0:1]
            al = als[0] if len(als) == 1 else jnp.concatenate(als[::-1], axis=1)
            a = jnp.exp((s - sp) - al)
            if mask is not None:
                a = jnp.where(mask, a, 0.0)
            acc_ref[...] += _dot_nt(a.astype(BF16), vtb)
            m_ref[...] = later
        else:
            if bias is not None:
                s = s + bias
            if mask is not None:
                s = jnp.where(mask, s, NEG_INF)
            m_old = m_ref[...]
            m_new = jnp.maximum(m_old, jnp.max(s, axis=-1, keepdims=True))
            p = jnp.exp(s - m_new)
            alpha = jnp.exp(m_old - m_new)
            l_ref[...] = alpha * l_ref[...] + jnp.sum(p, axis=-1, keepdims=True)
            acc_ref[...] = alpha * acc_ref[...] + _dot_nt(p.astype(BF16), vtb)
            m_ref[...] = m_new

    def rows_of(vecs):
        return jnp.concatenate([jnp.broadcast_to(v, (t_s, 128)) for v in vecs], axis=0)

    @pl.when(step == 0)
    def _first():
        q = q_ref[...]
        if nsa:
            parts = []
            for g in range(HKV_NSA):
                hm = lane_lo if g == 0 else jnp.logical_not(lane_lo)
                for r in range(GQA_R):
                    parts.append(jnp.where(hm, q[:, 128 * r:128 * (r + 1)], jnp.zeros((), BF16)))
        else:
            parts = [jnp.where(head_lanes(n, 256), q, jnp.zeros((), BF16)) for n in range(nh)]
        qbd_ref[...] = jnp.concatenate(parts, axis=0)
        acc_ref[...] = jnp.zeros_like(acc_ref)
        l_ref[...] = jnp.zeros_like(l_ref)
        m_ref[...] = jnp.zeros_like(m_ref) if mode == "sb" else jnp.full_like(m_ref, NEG_INF)
        key = _iota((rows, PAGE), 1)
        mask = (key < t_of_row) if mode == "sb" else (key <= t_of_row)
        bias = None
        if mode == "fox":
            cf_ref[...] = jnp.zeros_like(cf_ref)
            bias = -rows_of([rtn_ref[0, :, 512 + 128 * n:512 + 128 * (n + 1)] for n in range(nh)])
        if mode == "sb":
            dead_ref[0] = 0
        block(new_ref[0, 0:kw, :], new_ref[0, kw:2 * kw, :], mask, bias)

    live = (dead_ref[0] == 0) if mode == "sb" else True

    @pl.when((step > 0) & live)
    def _pages():
        bias = None
        mask = None
        if mode == "fox":
            later = cf_ref[...]
            pieces = []
            for g in reversed(range(g_n)):
                rt = rts[g]
                pieces.append(rows_of([rt[0, :, 128 * n:128 * (n + 1)] for n in range(nh)]) + later)
                later = later + rows_of([rt[0, :, 1024 + 128 * n:1024 + 128 * (n + 1)] for n in range(nh)])
            bias = jnp.concatenate(pieces[::-1], axis=1)
            cf_ref[...] = later
        if mode == "sel":
            nk = g_n * PAGE
            first_pos = (n_pages - g_n * step) * PAGE
            blk = (first_pos + _iota((2 * n_sp, nk), 1)) // SEL_BLOCK
            e = (_iota((2 * n_sp, nk), 0) % n_sp) == blk
            grp = _iota((rows, 2 * n_sp), 0) // (GQA_R * t_s) == _iota((rows, 2 * n_sp), 1) // n_sp
            smr = jnp.concatenate([sm_ref[...]] * nh, axis=0)
            smr = jnp.where(grp, smr, jnp.zeros((), BF16))
            mask = _dot(smr, e.astype(BF16)) > 0.5
        kt = jnp.concatenate([p[0, 0:kw, :] for p in pages], axis=1)
        vt = jnp.concatenate([p[0, kw:2 * kw, :] for p in pages], axis=1)
        block(kt, vt, mask, bias)
        if mode == "sb":
            dead_ref[0] = (jnp.min(m_ref[...]) > SB_DEAD).astype(jnp.int32)

    @pl.when(step == n_steps - 1)
    def _fin():
        o = acc_ref[...] if mode == "sb" else acc_ref[...] / l_ref[...]
        if nsa:
            for r in range(GQA_R):
                o0 = o[t_s * r:t_s * (r + 1), :]
                o1 = o[t_s * (GQA_R + r):t_s * (GQA_R + r + 1), :]
                o_ref[:, 128 * r:128 * (r + 1)] = jnp.where(lane_lo, o0, o1)
        else:
            out = jnp.zeros((t_s, 256), F32)
            for n in range(nh):
                out = out + jnp.where(head_lanes(n, 256), o[t_s * n:t_s * (n + 1), :], 0.0)
            o_ref[...] = out


def _dec(mode, q, newkv_t, cache_t, layer, page_tab, extra=None, rt_new=None, rt_cache=None):
    nsa = mode == "sel"
    bs, n_pages = page_tab.shape
    t_s = q.shape[0] // bs
    wq = q.shape[1]
    wkv = cache_t.shape[1]
    g_n = G_DEC
    n_steps = 1 + n_pages // g_n
    n_sp = n_pages * PAGE // SEL_BLOCK
    nh = H_NSA if nsa else H_SB
    rows = nh * t_s

    def page_map(g, base):
        def f(b, s, pt):
            return (base + pt[b * n_pages + n_pages - g_n * jnp.maximum(s, 1) + g], 0, 0)
        return f

    base = layer * (cache_t.shape[0] // 2)
    in_specs = [pl.BlockSpec((t_s, wq), lambda b, s, pt: (b, 0)),
                pl.BlockSpec((1, wkv, PAGE), lambda b, s, pt: (b, 0, 0))]
    in_specs += [pl.BlockSpec((1, wkv, PAGE), page_map(g, base)) for g in range(g_n)]
    args = [q, newkv_t] + [cache_t] * g_n
    if mode == "sb":
        in_specs.append(pl.BlockSpec((PAGE, PAGE), lambda b, s, pt: (0, 0)))
        args.append(extra)
    if mode == "fox":
        in_specs.append(pl.BlockSpec((1, 1, 1536), lambda b, s, pt: (b, 0, 0)))
        args.append(rt_new)
        in_specs += [pl.BlockSpec((1, 1, 1536), page_map(g, 0)) for g in range(g_n)]
        args += [rt_cache] * g_n
    if mode == "sel":
        in_specs.append(pl.BlockSpec((t_s, 2 * n_sp), lambda b, s, pt: (b, 0)))
        args.append(extra)
    scratch = [pltpu.VMEM((rows, 128 if nsa else 256), BF16), pltpu.VMEM((rows, 128 if nsa else 256), F32),
               pltpu.VMEM((rows, 1), F32), pltpu.VMEM((rows, 1), F32)]
    if mode == "fox":
        scratch.append(pltpu.VMEM((rows, 128), F32))
    if mode == "sb":
        scratch.append(pltpu.SMEM((1,), jnp.int32))
    return pl.pallas_call(
        functools.partial(_dec_kernel, mode, n_pages, n_sp),
        grid_spec=pltpu.PrefetchScalarGridSpec(
            num_scalar_prefetch=1, grid=(bs, n_steps), in_specs=in_specs,
            out_specs=pl.BlockSpec((t_s, wq), lambda b, s, pt: (b, 0)), scratch_shapes=scratch),
        out_shape=jax.ShapeDtypeStruct((bs * t_s, wq), F32),
        compiler_params=_cparams(("parallel", "arbitrary")),
        name="dec_" + mode,
    )(page_tab.reshape(-1), *args)


def _cmp_s_kernel(n_pages, pt_ref, q_ref, *refs):
    g_n = G_PAGES
    pages = refs[0:g_n]
    wph_ref, wpl_ref, o_ref, sm_ref, kc_ref = refs[g_n:g_n + 5]
    t_s = q_ref.shape[0]
    n_sp = n_pages * PAGE // SEL_BLOCK
    n_c = 2 * n_sp
    per = g_n * PAGE // SEL_BLOCK
    step = pl.program_id(1)
    x = jnp.concatenate([p[0] for p in pages], axis=1)
    xh, xl = _split2(x)
    wh, wl = wph_ref[...], wpl_ref[...]
    pooled_t = (_dot(xh, wh) + _dot(xl, wh) + _dot(xh, wl)).T
    lane = _iota((1, 256), 1)
    pt = jnp.zeros((2 * per, 256), F32)
    for r in range(4):
        own = (lane >= HEAD_DIM * r) & (lane < HEAD_DIM * (r + 1))
        pt = pt + jnp.where(own, pooled_t[2 * per * r:2 * per * (r + 1), :], 0.0)
    off = pl.multiple_of(step * per, per)
    kc_ref[pl.ds(off, per), :] = pt[0:per, :]
    kc_ref[pl.ds(n_sp + off, per), :] = pt[per:2 * per, :]

    @pl.when(step == pl.num_programs(1) - 1)
    def _fin():
        kcb = kc_ref[:, 0:128].astype(BF16)
        vcb = kc_ref[:, 128:256].astype(BF16)
        lane_lo = _iota((1, LANES), 1) < HEAD_DIM
        q = q_ref[...]
        imps = []
        outs = []
        for g in range(HKV_NSA):
            hm = lane_lo if g == 0 else jnp.logical_not(lane_lo)
            qg = jnp.concatenate([jnp.where(hm, q[:, 128 * r:128 * (r + 1)], jnp.zeros((), BF16))
                                  for r in range(GQA_R)], axis=0)
            s = _dot_nt(qg, kcb)
            e = jnp.exp(s - jnp.max(s, axis=-1, keepdims=True))
            p = e / jnp.maximum(jnp.sum(e, axis=-1, keepdims=True), TINY)
            outs.append(_dot(p.astype(BF16), vcb))
            imp = jnp.zeros((t_s, n_sp), F32)
            for r in range(GQA_R):
                pr = p[t_s * r:t_s * (r + 1), :]
                imp = imp + (pr[:, 0:n_sp] + pr[:, n_sp:n_c])
            imps.append(imp)
        for r in range(GQA_R):
            o_ref[:, 128 * r:128 * (r + 1)] = jnp.where(
                lane_lo, outs[0][t_s * r:t_s * (r + 1), :], outs[1][t_s * r:t_s * (r + 1), :])
        imp = jnp.concatenate(imps, axis=0)
        jb = _iota((2 * t_s, n_sp), 1)
        sc = jnp.where((jb == 0) | (jb == n_sp - 1), FORCE_SCORE, imp)
        chosen = jnp.zeros(sc.shape, F32)
        for _ in range(min(SEL_TOPK - 1, n_sp)):
            idx = jnp.argmax(sc, axis=1).astype(jnp.int32)[:, None]
            hit = jb == idx
            chosen = jnp.where(hit, 1.0, chosen)
            sc = jnp.where(hit, -jnp.inf, sc)
        sm_ref[...] = jnp.concatenate([chosen[0:t_s], chosen[t_s:2 * t_s]], axis=1).astype(BF16)


def _cmp_s(nsaq, cache_t, layer, page_tab, lw):
    bs, n_pages = page_tab.shape
    t_s = nsaq.shape[0] // bs
    g_n = G_PAGES
    n_sp = n_pages * PAGE // SEL_BLOCK
    base = layer * (cache_t.shape[0] // 2)

    def page_map(g):
        return lambda b, s, pt: (base + pt[b * n_pages + g_n * s + g], 0, 0)

    const2 = lambda b, s, pt: (0, 0)
    return pl.pallas_call(
        functools.partial(_cmp_s_kernel, n_pages),
        grid_spec=pltpu.PrefetchScalarGridSpec(
            num_scalar_prefetch=1, grid=(bs, n_pages // g_n),
            in_specs=[pl.BlockSpec((t_s, 512), lambda b, s, pt: (b, 0))]
                     + [pl.BlockSpec((1, 256, PAGE), page_map(g)) for g in range(g_n)]
                     + [pl.BlockSpec((g_n * PAGE, 128), const2), pl.BlockSpec((g_n * PAGE, 128), const2)],
            out_specs=[pl.BlockSpec((t_s, 512), lambda b, s, pt: (b, 0)),
                       pl.BlockSpec((t_s, 2 * n_sp), lambda b, s, pt: (b, 0))],
            scratch_shapes=[pltpu.VMEM((2 * n_sp, 256), F32)]),
        out_shape=[jax.ShapeDtypeStruct((bs * t_s, 512), F32), jax.ShapeDtypeStruct((bs * t_s, 2 * n_sp), BF16)],
        compiler_params=_cparams(("parallel", "arbitrary")),
        name="cmp_s",
    )(page_tab.reshape(-1), nsaq, *([cache_t] * g_n), lw["wph"], lw["wpl"])


def _win_s_kernel(q_ref, st_ref, new_ref, o_ref):
    t_s = q_ref.shape[0]
    n_w = st_ref.shape[2]
    rows = H_NSA * t_s
    lane_lo = _iota((1, LANES), 1) < HEAD_DIM
    q = q_ref[...]
    parts = []
    for g in range(HKV_NSA):
        hm = lane_lo if g == 0 else jnp.logical_not(lane_lo)
        for r in range(GQA_R):
            parts.append(jnp.where(hm, q[:, 128 * r:128 * (r + 1)], jnp.zeros((), BF16)))
    qbd = jnp.concatenate(parts, axis=0)
    t_of_row = _iota((rows, 1), 0) % t_s
    jp = _iota((rows, n_w), 1)
    mask_p = (jp > t_of_row + (n_w - WINDOW)) & (jp <= t_of_row + n_w)
    jn = _iota((rows, PAGE), 1)
    mask_n = jn <= t_of_row
    s_p = jnp.where(mask_p, _dot(qbd, st_ref[0, 0:128, :].astype(BF16)), NEG_INF)
    s_n = jnp.where(mask_n, _dot(qbd, new_ref[0, 0:128, :].astype(BF16)), NEG_INF)
    mx = jnp.maximum(jnp.max(s_p, axis=-1, keepdims=True), jnp.max(s_n, axis=-1, keepdims=True))
    p_p = jnp.exp(s_p - mx)
    p_n = jnp.exp(s_n - mx)
    den = jnp.sum(p_p, axis=-1, keepdims=True) + jnp.sum(p_n, axis=-1, keepdims=True)
    o = (_dot_nt(p_p.astype(BF16), st_ref[0, 128:256, :].astype(BF16))
         + _dot_nt(p_n.astype(BF16), new_ref[0, 128:256, :].astype(BF16))) / den
    for r in range(GQA_R):
        o_ref[:, 128 * r:128 * (r + 1)] = jnp.where(
            lane_lo, o[t_s * r:t_s * (r + 1), :], o[t_s * (GQA_R + r):t_s * (GQA_R + r + 1), :])


def _win_s(nsaq, state_t, layer, newkv_t, bs):
    t_s = nsaq.shape[0] // bs
    n_w = state_t.shape[2]
    return pl.pallas_call(
        _win_s_kernel,
        grid=(bs,),
        in_specs=[pl.BlockSpec((t_s, 512), lambda b: (b, 0)),
                  pl.BlockSpec((1, 256, n_w), lambda b: (layer * bs + b, 0, 0)),
                  pl.BlockSpec((1, 256, PAGE), lambda b: (b, 0, 0))],
        out_specs=pl.BlockSpec((t_s, 512), lambda b: (b, 0)),
        out_shape=jax.ShapeDtypeStruct((bs * t_s, 512), F32),
        compiler_params=_cparams(("parallel",)),
        name="win_s",
    )(nsaq, state_t, newkv_t)


def _merge_kernel(x_ref, g_ref, osb_ref, ofox_ref, ocmp_ref, osel_ref, owin_ref, misc_ref,
                  wmg_ref, wsb_ref, wfox_ref, wnsa_ref, wo_ref, ex_ref, y_ref):
    x = x_ref[...]
    h = x * lax.rsqrt(jnp.mean(x * x, axis=-1, keepdims=True) + EPS) * g_ref[...]
    hb = h.astype(BF16)
    mhi, mlo = _split2(misc_ref[...])

    def gate(i):
        ex = ex_ref[:, 512 * i:512 * (i + 1)]
        return _dot(mhi, ex) + _dot(mlo, ex)

    o_nsa = gate(0) * ocmp_ref[...] + gate(1) * osel_ref[...] + gate(2) * owin_ref[...]
    d = x.shape[1]

    def mg(i):
        return jax.nn.sigmoid(_dot_nt(hb, wmg_ref[d * i:d * (i + 1), :]))

    mixed = (mg(0) * _dot(osb_ref[...].astype(BF16), wsb_ref[...])
             + mg(1) * _dot(ofox_ref[...].astype(BF16), wfox_ref[...])
             + mg(2) * _dot(o_nsa.astype(BF16), wnsa_ref[...]))
    y_ref[...] = x + _dot(mixed.astype(BF16), wo_ref[...])


def _merge(x2, o_sb, o_fox, o_cmp, o_sel, o_win, misc, lw, tm):
    n, d = x2.shape
    row = lambda i: (i, 0)
    const = lambda i: (0, 0)
    return pl.pallas_call(
        _merge_kernel,
        grid=(n // tm,),
        in_specs=[pl.BlockSpec((tm, d), row), pl.BlockSpec((1, d), const),
                  pl.BlockSpec((tm, 256), row), pl.BlockSpec((tm, 256), row),
                  pl.BlockSpec((tm, 512), row), pl.BlockSpec((tm, 512), row), pl.BlockSpec((tm, 512), row),
                  pl.BlockSpec((tm, 128), row),
                  pl.BlockSpec((N_MG, d), const), pl.BlockSpec((256, d), const), pl.BlockSpec((256, d), const),
                  pl.BlockSpec((512, d), const), pl.BlockSpec((d, d), const), pl.BlockSpec((128, 1536), const)],
        out_specs=pl.BlockSpec((tm, d), row),
        out_shape=jax.ShapeDtypeStruct((n, d), F32),
        compiler_params=_cparams(("parallel",)),
        name="merge",
    )(x2, lw["attn_norm"], o_sb, o_fox, o_cmp, o_sel, o_win, misc,
      lw["w_mg"], lw["w_br_sb"], lw["w_br_fox"], lw["w_br_nsa"], lw["w_o"], lw["ex"])


FF_CHUNK = 768


def _ffn_kernel(x_ref, g_ref, hist_ref, wup_ref, cw_ref, cb_ref, wdn_ref, y_ref, cs_ref, prev_ref):
    tm = x_ref.shape[0]
    d_ff = wdn_ref.shape[0]
    ti = pl.program_id(1)

    @pl.when(ti == 0)
    def _hist():
        prev_ref[...] = hist_ref[0]

    x = x_ref[...]
    h = x * lax.rsqrt(jnp.mean(x * x, axis=-1, keepdims=True) + EPS) * g_ref[...]
    hb = h.astype(BF16)
    rowi = _iota((tm, 1), 0)
    y = x
    for c in range(0, d_ff, FF_CHUNK):
        gate = _dot(hb, wup_ref[:, c:c + FF_CHUNK])
        up = _dot(hb, wup_ref[:, d_ff + c:d_ff + c + FF_CHUNK])
        p6 = prev_ref[6:7, c:c + FF_CHUNK]
        p7 = prev_ref[7:8, c:c + FF_CHUNK]
        g1 = jnp.where(rowi == 0, p7, pltpu.roll(gate, 1, 0))
        g2 = jnp.where(rowi == 0, p6, jnp.where(rowi == 1, p7, pltpu.roll(gate, 2, 0)))
        conv = (cb_ref[:, c:c + FF_CHUNK] + g2 * cw_ref[0:1, c:c + FF_CHUNK]
                + g1 * cw_ref[1:2, c:c + FF_CHUNK] + gate * cw_ref[2:3, c:c + FF_CHUNK])
        act = conv * jax.nn.sigmoid(conv) * up
        y = y + _dot(act.astype(BF16), wdn_ref[c:c + FF_CHUNK, :])
        tail = gate[tm - 8:tm, :]
        prev_ref[:, c:c + FF_CHUNK] = tail
        cs_ref[0, :, c:c + FF_CHUNK] = tail
    y_ref[...] = y


def _ffn(x2, hist, lw, nb, tm):
    n, d = x2.shape
    nt = n // nb // tm
    d_ff = lw["w_down"].shape[0]
    row = lambda b, i: (b * nt + i, 0)
    const = lambda b, i: (0, 0)
    hmap = (lambda b, i: (b, 0, 0)) if hist.shape[0] == nb else (lambda b, i: (0, 0, 0))
    once = dict(pipeline_mode=pl.Buffered(1))
    return pl.pallas_call(
        _ffn_kernel,
        grid=(nb, nt),
        in_specs=[pl.BlockSpec((tm, d), row), pl.BlockSpec((1, d), const), pl.BlockSpec((1, 8, d_ff), hmap),
                  pl.BlockSpec((d, 2 * d_ff), const, **once), pl.BlockSpec((CONV_W, d_ff), const),
                  pl.BlockSpec((1, d_ff), const), pl.BlockSpec((d_ff, d), const, **once)],
        out_specs=[pl.BlockSpec((tm, d), row), pl.BlockSpec((1, 8, d_ff), lambda b, i: (b, 0, 0))],
        out_shape=[jax.ShapeDtypeStruct((n, d), F32), jax.ShapeDtypeStruct((nb, 8, d_ff), F32)],
        scratch_shapes=[pltpu.VMEM((8, d_ff), F32)],
        compiler_params=_cparams(("parallel", "arbitrary")),
        name="ffn",
    )(x2, lw["ffn_norm"], hist, lw["w_up"], lw["conv_w"], lw["conv_b"], lw["w_down"])


def _consts():
    bd = (np.arange(256)[:, None] // HEAD_DIM == np.arange(256)[None, :] // HEAD_DIM)
    tri_rows = np.arange(256)[:, None] < np.arange(256)[None, :]
    tri128 = np.arange(PAGE)[:, None] > np.arange(PAGE)[None, :]
    hd, ps = np.arange(512) // 128, np.arange(512) % 128
    same = hd[:, None] == hd[None, :]
    c_r = same & (ps[:, None] > ps[None, :])
    c_w = same & (ps[:, None] <= ps[None, :])
    cmat = np.concatenate([c_r, c_w, same], axis=1)
    ex = np.zeros((128, 1536), np.float32)
    for br in range(3):
        for g in range(HKV_NSA):
            for r in range(GQA_R):
                m = 2 * r + g
                ex[H_FOX + br * 8 + g * GQA_R + r, 512 * br + 64 * m:512 * br + 64 * (m + 1)] = 1.0
    as_bf = lambda a: jnp.asarray(a.astype(np.float32), BF16)
    return dict(bd=as_bf(bd), tri_rows=as_bf(tri_rows), tri128=as_bf(tri128), cmat=as_bf(cmat), ex=as_bf(ex))


def _pool_matrices(pool):
    n_pos = G_PAGES * PAGE
    pos = np.arange(n_pos)
    blk = pos // CMP_BLOCK
    col = blk // 2 + (n_pos // SEL_BLOCK) * (blk % 2)
    w = jnp.zeros((n_pos, 128), F32)
    for kv in range(2):
        for g in range(HKV_NSA):
            r = 2 * kv + g
            place = jnp.asarray((32 * r + col[:, None] == np.arange(128)[None, :]).astype(np.float32))
            w = w + place * pool[kv, :, g][pos % CMP_BLOCK][:, None]
    hi = w.astype(BF16)
    lo = (w - hi.astype(F32)).astype(BF16)
    return hi, lo


def _layer_weights(l, cst, attn_norm, w_in, fox_forget_bias, fox_qk_gain, nsa_qk_gain, nsa_cmp_pool,
                   w_br_sb, w_br_fox, w_br_nsa, w_o, ffn_norm, w_up, conv_w, conv_b, w_down):
    w = w_in[l].T
    d = w.shape[1]
    o_foxf = 6 * 256
    o_nsaq = o_foxf + H_FOX
    o_kv = o_nsaq + H_NSA * HEAD_DIM
    o_gate = o_kv + 6 * 128
    o_mg = o_gate + 3 * H_NSA
    head_order = [(m % 2) * GQA_R + m // 2 for m in range(H_NSA)]
    nsaq_rows = jnp.concatenate([w[o_nsaq + 64 * hh:o_nsaq + 64 * (hh + 1)] for hh in head_order], axis=0)
    misc = jnp.concatenate([w[o_foxf:o_foxf + H_FOX], w[o_gate:o_gate + 3 * H_NSA],
                            jnp.zeros((128 - H_FOX - 3 * H_NSA, d), w.dtype)], axis=0)
    w_a = jnp.concatenate([w[0:o_foxf], nsaq_rows, w[o_kv:o_gate], misc], axis=0).astype(BF16)
    tile = lambda v, n: jnp.tile(v, n)[None, :]
    gains = jnp.concatenate([
        tile(fox_qk_gain[l, 0], 4), tile(fox_qk_gain[l, 1], 4), tile(nsa_qk_gain[l, 0], 4),
        tile(nsa_qk_gain[l, 1], 4), tile(nsa_qk_gain[l, 2], 4), tile(nsa_qk_gain[l, 3], 4),
        jnp.zeros((2, 256), F32)], axis=0)
    fbias = jnp.concatenate([fox_forget_bias[l], jnp.zeros((128 - H_FOX,), F32)])[None, :]
    pool = nsa_cmp_pool[l]
    lanes = lambda p: jnp.repeat(p, HEAD_DIM, axis=1)
    pw = jnp.concatenate([lanes(pool[0]), lanes(pool[1])], axis=1)
    zero = jnp.zeros_like(pw)
    wph, wpl = _pool_matrices(pool)
    nsa_rows = jnp.concatenate([w_br_nsa[l, 64 * hh:64 * (hh + 1)] for hh in head_order], axis=0)
    return dict(
        attn_norm=attn_norm[l][None, :], w_a=w_a, w_mg=w[o_mg:].astype(BF16), bd=cst["bd"], gains=gains,
        fbias=fbias, pwe=jnp.concatenate([pw, zero], axis=0), pwo=jnp.concatenate([zero, pw], axis=0),
        wph=wph, wpl=wpl,
        w_br_sb=w_br_sb[l].astype(BF16), w_br_fox=w_br_fox[l].astype(BF16), w_br_nsa=nsa_rows.astype(BF16),
        w_o=w_o[l].astype(BF16), ex=cst["ex"], ffn_norm=ffn_norm[l][None, :], w_up=w_up[l].astype(BF16),
        conv_w=conv_w[l], conv_b=conv_b[l][None, :], w_down=w_down[l].astype(BF16))


def _rope_tables(pos):
    half = ROPE_DIM // 2
    inv = ROPE_THETA ** (-jnp.arange(half, dtype=F32) * (2.0 / ROPE_DIM))
    ang = pos.astype(F32)[:, None] * inv[None, :]
    cos, sin = jnp.cos(ang), jnp.sin(ang)
    t = pos.shape[0]
    one = jnp.ones((t, HEAD_DIM - ROPE_DIM), F32)
    zero8 = jnp.zeros((t, half), F32)
    zero48 = jnp.zeros((t, HEAD_DIM - ROPE_DIM), F32)
    two = lambda a: jnp.concatenate([a, a], axis=1)
    return (two(jnp.concatenate([cos, cos, one], axis=1)),
            two(jnp.concatenate([-sin, zero8, zero48], axis=1)),
            two(jnp.concatenate([zero8, sin, zero48], axis=1)))


def _new_block_t(a, bs):
    t = a.shape[0] // bs
    at = jnp.swapaxes(a.reshape(bs, t, a.shape[1]), 1, 2)
    return jnp.pad(at, ((0, 0), (0, 0), (0, PAGE - t)))


def _feature_major(c):
    depth, n, npos = c.shape[0:3]
    ct = jnp.moveaxis(c, 2, -1)
    return ct.reshape(depth * n, -1, npos)


def _prompt_layer(x2, b, t, lw, cst, tabs):
    tm = 512 if t % 512 == 0 else 256
    (sbk, foxk, nsak, nsaq, cmpkv, misc,
     sbqt, foxqt, nsaqt, sbkvt, foxkvt, cmpt, selt, wint, lft) = _proj(x2, lw, *tabs, tm, nb=b)
    tq = 512 if t % 512 == 0 else 256
    o_sb = _attn("sb", sbqt, sbk, sbkvt, b, t, tri=cst["tri_rows"], tq=tq)
    o_fox = _attn("fox", foxqt, foxk, foxkvt, b, t, extra=_fox_bias(misc, cst["tri_rows"], tq), tq=tq)
    o_cmp, selm = _cmp_p(nsaq, cmpkv, lw, b, t)
    o_sel = _attn("sel", nsaqt, nsak, selt, b, t, kcol=1, extra=selm, tq=tq)
    o_win = _attn("win", nsaqt, nsak, wint, b, t, kcol=2, tq=256)
    x_mid = _merge(x2, o_sb, o_fox, o_cmp, o_sel, o_win, misc, lw, 256)
    d_ff = lw["w_down"].shape[0]
    y, cs = _ffn(x_mid, jnp.zeros((1, 8, d_ff), F32), lw, b, 256)
    n_w = min(WINDOW, t)
    state = (sbkvt, foxkvt, lft[:, 0:H_FOX], cmpt, selt, wint[:, :, t - n_w:], cs[:, 8 - (CONV_W - 1):])
    return y, state


def _sample_layer(x2, bs, l, lw, cst, tabs, page_tab, c_sb, c_fox, c_logf, c_cmp, c_sel, st_win, st_conv):
    n = x2.shape[0]
    t_s = n // bs
    sbq, sbkv, foxq, foxkv, nsaq, cmpkv, selkv, winkv, misc = _proj(x2, lw, *tabs, n)
    logf_new = misc[:, 0:H_FOX]
    lf_new_pages = _new_block_t(logf_new, bs).reshape(bs, PAGE * H_FOX)
    rt_new = _logf_sums(lf_new_pages, cst["cmat"]).reshape(bs, 1, 1536)
    n_pool = c_logf.shape[0] // 2
    rt_cache = _logf_sums(c_logf[l * n_pool:(l + 1) * n_pool].reshape(n_pool, PAGE * H_FOX), cst["cmat"])
    rt_cache = rt_cache.reshape(n_pool, 1, 1536)
    o_sb = _dec("sb", sbq, _new_block_t(sbkv, bs), c_sb, l, page_tab, cst["tri128"])
    o_fox = _dec("fox", foxq, _new_block_t(foxkv, bs), c_fox, l, page_tab, rt_new=rt_new, rt_cache=rt_cache)
    o_cmp, selm = _cmp_s(nsaq, c_cmp, l, page_tab, lw)
    o_sel = _dec("sel", nsaq, _new_block_t(selkv, bs), c_sel, l, page_tab, selm)
    o_win = _win_s(nsaq, st_win, l, _new_block_t(winkv, bs), bs)
    x_mid = _merge(x2, o_sb, o_fox, o_cmp, o_sel, o_win, misc, lw, n)
    hist = jnp.pad(st_conv, ((0, 0), (8 - (CONV_W - 1), 0), (0, 0)))
    y, cs = _ffn(x_mid, hist, lw, bs, t_s)
    state = (sbkv, foxkv, logf_new, cmpkv, selkv, winkv, cs[:, 8 - (CONV_W - 1):])
    return y, state


def kernel(x_prompt, x_sample, cache_sb_kv, cache_fox_kv, cache_fox_logf, cache_nsa_cmp_kv, cache_nsa_sel_kv, state_nsa_win_kv, state_ffn_conv, page_table, attn_norm, w_in, fox_forget_bias, fox_qk_gain, nsa_qk_gain, nsa_cmp_pool, w_br_sb, w_br_fox, w_br_nsa, w_o, ffn_norm, w_up, conv_w, conv_b, w_down):
    b, t, d = x_prompt.shape
    bs, t_s, _ = x_sample.shape
    depth = w_in.shape[0]
    n_pages = page_table.shape[1]
    past = n_pages * PAGE
    n_w = state_nsa_win_kv.shape[2]
    assert t % 256 == 0 and t_s == 8 and n_pages % G_DEC == 0 and past >= WINDOW and depth == 2
    cst = _consts()
    tabs_p = _rope_tables(jnp.arange(t, dtype=jnp.int32))
    tabs_s = tuple(jnp.tile(a, (bs, 1)) for a in _rope_tables(past + jnp.arange(t_s, dtype=jnp.int32)))
    c_sb, c_fox = _feature_major(cache_sb_kv), _feature_major(cache_fox_kv)
    c_cmp, c_sel = _feature_major(cache_nsa_cmp_kv), _feature_major(cache_nsa_sel_kv)
    c_logf = _feature_major(cache_fox_logf)
    st_win = _feature_major(state_nsa_win_kv)
    hp = x_prompt.reshape(b * t, d)
    hs = x_sample.reshape(bs * t_s, d)
    st_p, st_s = [], []
    for l in range(depth):
        lw = _layer_weights(l, cst, attn_norm, w_in, fox_forget_bias, fox_qk_gain, nsa_qk_gain, nsa_cmp_pool,
                            w_br_sb, w_br_fox, w_br_nsa, w_o, ffn_norm, w_up, conv_w, conv_b, w_down)
        hp, new_p = _prompt_layer(hp, b, t, lw, cst, tabs_p)
        hs, new_s = _sample_layer(hs, bs, l, lw, cst, tabs_s, page_table, c_sb, c_fox, c_logf, c_cmp, c_sel,
                                  st_win, state_ffn_conv[l])
        st_p.append(new_p)
        st_s.append(new_s)

    def stack_p(i, shape):
        a = jnp.stack([s[i] for s in st_p], axis=0)
        a = a.reshape((depth, b) + shape + (a.shape[-1],))
        return jnp.moveaxis(a, -1, 2)

    def stack_s(i, shape):
        return jnp.stack([s[i].reshape((bs, t_s) + shape) for s in st_s], axis=0)

    kv = lambda h: (2, h, HEAD_DIM)
    win_new = stack_s(5, kv(HKV_NSA))
    win_s = jnp.concatenate([state_nsa_win_kv[:, :, t_s:], win_new], axis=2)[:, :, -n_w:]
    return (hp.reshape(b, t, d), hs.reshape(bs, t_s, d),
            stack_p(0, kv(H_SB)), stack_s(0, kv(H_SB)),
            stack_p(1, kv(H_FOX)), stack_s(1, kv(H_FOX)),
            stack_p(2, (H_FOX,)), stack_s(2, (H_FOX,)),
            stack_p(3, kv(HKV_NSA)), stack_s(3, kv(HKV_NSA)),
            stack_p(4, kv(HKV_NSA)), stack_s(4, kv(HKV_NSA)),
            stack_p(5, kv(HKV_NSA)), win_s,
            jnp.stack([s[6] for s in st_p], axis=0), jnp.stack([s[6] for s in st_s], axis=0))
```
